```python
import math
import jax, jax.numpy as jnp
from jax import lax
import numpy as np

D_MODEL = 4096
BATCH = 1
SEQ = 8192
DEPTH = 4

CTX_LEN = 256
GRID_W = 64
N_MIXERS = 3
HEAD_DIM = 128
N_HEADS_A = D_MODEL // HEAD_DIM
N_KV_HEADS_A = N_HEADS_A // 4
N_HEADS_C = D_MODEL // (2 * HEAD_DIM)
DIFF_V_DIM = 2 * HEAD_DIM
ROPE_THETA = 10000.0
Q_BLOCK = 128
POOL_WINDOWS = (2, 4, 8, 16)
N_POOL_GROUPS = len(POOL_WINDOWS)
POOL_GROUP_DIM = D_MODEL // N_POOL_GROUPS
N_EXPERT_GROUPS = 4
EXPERTS_PER_GROUP = 4
N_EXPERTS = N_EXPERT_GROUPS * EXPERTS_PER_GROUP
D_EXPERT = 3 * D_MODEL // 32
TOP_K_INNER = 2
N_LAYERS_A = (DEPTH + 2) // 3
N_LAYERS_B = (DEPTH + 1) // 3
N_LAYERS_C = DEPTH // 3
DN_ALPHA = (2 * DEPTH) ** 0.25
DN_BETA = (8 * DEPTH) ** -0.25
LN_EPS = 1e-6
RMS_EPS = 1e-6

kernel_name = 'hybrid_interleaved_gqa_pool_diffattn_hmoe_dit'


def layer_norm(x, g, b):
    xf = x.astype(jnp.float32)
    mu = jnp.mean(xf, axis=-1, keepdims=True)
    var = jnp.mean(jnp.square(xf - mu), axis=-1, keepdims=True)
    return ((xf - mu) * lax.rsqrt(var + LN_EPS) * g + b).astype(x.dtype)


def rms_norm(x, g):
    xf = x.astype(jnp.float32)
    y = xf * lax.rsqrt(jnp.mean(jnp.square(xf), axis=-1, keepdims=True) + RMS_EPS)
    return (y * g).astype(x.dtype)


def axial_rope_tables(n_tokens, dtype):
    rows = n_tokens // GRID_W
    row = jnp.broadcast_to(jnp.arange(rows, dtype=jnp.float32)[:, None], (rows, GRID_W)).reshape(-1)
    col = jnp.broadcast_to(jnp.arange(GRID_W, dtype=jnp.float32)[None, :], (rows, GRID_W)).reshape(-1)
    n_freq = HEAD_DIM // 4
    inv_freq = ROPE_THETA ** (-jnp.arange(n_freq, dtype=jnp.float32) / n_freq)
    ang = jnp.stack([row[:, None] * inv_freq, col[:, None] * inv_freq], axis=1)
    return jnp.cos(ang).astype(dtype), jnp.sin(ang).astype(dtype)


def apply_axial_rope(x, cos, sin):
    b, n, h, hd = x.shape
    xr = x.reshape(b, n, h, 2, 2, hd // 4)
    x1, x2 = xr[..., 0, :], xr[..., 1, :]
    cs, sn = cos[None, :, None], sin[None, :, None]
    out = jnp.stack([x1 * cs - x2 * sn, x2 * cs + x1 * sn], axis=-2)
    return out.reshape(b, n, h, hd)


def sweep_query_blocks(attend, qs):
    b, n = qs[0].shape[:2]
    nb = n // Q_BLOCK
    blocks = tuple(jnp.swapaxes(q.reshape((b, nb, Q_BLOCK) + q.shape[2:]), 0, 1) for q in qs)
    out = lax.map(lambda qb: attend(*qb), blocks)
    out = jnp.swapaxes(out, 0, 1)
    return out.reshape((b, n) + out.shape[3:])


def gqa_attend(q, k, v):
    b, nq, h, hd = q.shape
    kvh = k.shape[2]
    qg = q.reshape(b, nq, kvh, h // kvh, hd)
    s = jnp.einsum('bqhgd,bshd->bhgqs', qg, k).astype(jnp.float32) * (1.0 / math.sqrt(hd))
    p = jax.nn.softmax(s, axis=-1).astype(v.dtype)
    o = jnp.einsum('bhgqs,bshd->bqhgd', p, v)
    return o.reshape(b, nq, h, hd)


def gqa_mixer(h, hc, rope_cos, rope_sin, wq, wk, wv, wo, qn_g, kn_g, with_ctx_out):
    b, n, _ = h.shape
    lc = hc.shape[1]
    q = apply_axial_rope(rms_norm((h @ wq).reshape(b, n, N_HEADS_A, HEAD_DIM), qn_g), rope_cos, rope_sin)
    k = apply_axial_rope(rms_norm((h @ wk).reshape(b, n, N_KV_HEADS_A, HEAD_DIM), kn_g), rope_cos, rope_sin)
    v = (h @ wv).reshape(b, n, N_KV_HEADS_A, HEAD_DIM)
    kc = rms_norm((hc @ wk).reshape(b, lc, N_KV_HEADS_A, HEAD_DIM), kn_g)
    vc = (hc @ wv).reshape(b, lc, N_KV_HEADS_A, HEAD_DIM)
    k_all = jnp.concatenate([kc, k], axis=1)
    v_all = jnp.concatenate([vc, v], axis=1)
    o = sweep_query_blocks(lambda qb: gqa_attend(qb, k_all, v_all), (q,))
    y = o.reshape(b, n, -1) @ wo
    if not with_ctx_out:
        return y, None
    qc = rms_norm((hc @ wq).reshape(b, lc, N_HEADS_A, HEAD_DIM), qn_g)
    yc = gqa_attend(qc, kc, vc).reshape(b, lc, -1) @ wo
    return y, yc


def diff_attend(q1, q2, k1, k2, v, lam):
    scale = 1.0 / math.sqrt(q1.shape[-1])
    s1 = jnp.einsum('bqhd,bshd->bhqs', q1, k1).astype(jnp.float32) * scale
    s2 = jnp.einsum('bqhd,bshd->bhqs', q2, k2).astype(jnp.float32) * scale
    p = jax.nn.softmax(s1, axis=-1) - lam * jax.nn.softmax(s2, axis=-1)
    return jnp.einsum('bhqs,bshe->bqhe', p.astype(v.dtype), v)


def diff_mixer(h, hc, rope_cos, rope_sin, wq, wk, wv, wo, lq1, lk1, lq2, lk2, subln_g, lam_init, with_ctx_out):
    b, n, _ = h.shape
    lc = hc.shape[1]
    lam = (jnp.exp(jnp.sum(lq1.astype(jnp.float32) * lk1.astype(jnp.float32)))
           - jnp.exp(jnp.sum(lq2.astype(jnp.float32) * lk2.astype(jnp.float32))) + lam_init)

    def split_qk(z, w, length, rotate):
        t = (z @ w).reshape(b, length, 2 * N_HEADS_C, HEAD_DIM)
        if rotate:
            t = apply_axial_rope(t, rope_cos, rope_sin)
        t = t.reshape(b, length, N_HEADS_C, 2, HEAD_DIM)
        return t[:, :, :, 0], t[:, :, :, 1]

    def finish(o, length):
        o = rms_norm(o, subln_g) * (1.0 - lam_init)
        return o.reshape(b, length, -1) @ wo

    q1, q2 = split_qk(h, wq, n, True)
    k1, k2 = split_qk(h, wk, n, True)
    v = (h @ wv).reshape(b, n, N_HEADS_C, DIFF_V_DIM)
    kc1, kc2 = split_qk(hc, wk, lc, False)
    vc = (hc @ wv).reshape(b, lc, N_HEADS_C, DIFF_V_DIM)
    k1_all = jnp.concatenate([kc1, k1], axis=1)
    k2_all = jnp.concatenate([kc2, k2], axis=1)
    v_all = jnp.concatenate([vc, v], axis=1)
    o = sweep_query_blocks(lambda a, c2: diff_attend(a, c2, k1_all, k2_all, v_all, lam), (q1, q2))
    y = finish(o, n)
    if not with_ctx_out:
        return y, None
    qc1, qc2 = split_qk(hc, wq, lc, False)
    yc = finish(diff_attend(qc1, qc2, kc1, kc2, vc, lam), lc)
    return y, yc


def pool_mixer(h, pool_w, pool_b, pool_scale):
    b, n, d = h.shape
    hg = h.reshape(b, n, N_POOL_GROUPS, POOL_GROUP_DIM)
    csum = jnp.cumsum(hg.astype(jnp.float32), axis=1)
    prefix = jnp.concatenate([jnp.zeros((b, 1, N_POOL_GROUPS, POOL_GROUP_DIM), jnp.float32), csum], axis=1)
    t = jnp.arange(n)[:, None]
    w = jnp.array(POOL_WINDOWS, dtype=jnp.int32)[None, :]
    lo = jnp.clip(t - w // 2, 0, n - 1)
    hi = jnp.clip(t + w // 2 - 1, 0, n - 1)
    gi = jnp.arange(N_POOL_GROUPS)[None, :]
    win_sum = prefix[:, hi + 1, gi] - prefix[:, lo, gi]
    mean = win_sum / (hi - lo + 1).astype(jnp.float32)[None, :, :, None]
    y = (mean - hg.astype(jnp.float32)).astype(h.dtype)
    y = jnp.einsum('bngc,gce->bnge', y, pool_w) + pool_b
    return y.reshape(b, n, d) * pool_scale


def hier_moe(h, rg_w, rg_b, re_w, re_b, w1, w3, w2):
    t = h.shape[0]
    g_logits = (h @ rg_w + rg_b).astype(jnp.float32)
    g_prob = jax.nn.softmax(g_logits, axis=-1)
    _, g_top = lax.top_k(g_logits, 1)
    p_g = jnp.take_along_axis(g_prob, g_top, axis=-1)
    e_logits = (h @ re_w + re_b).astype(jnp.float32).reshape(t, N_EXPERT_GROUPS, EXPERTS_PER_GROUP)
    e_sel = jnp.take_along_axis(e_logits, g_top[:, :, None], axis=1)[:, 0]
    top_v, top_i = lax.top_k(e_sel, TOP_K_INNER)
    top_w = jax.nn.softmax(top_v, axis=-1) * p_g
    expert_idx = g_top * EXPERTS_PER_GROUP + top_i
    gate = jnp.sum(jax.nn.one_hot(expert_idx, N_EXPERTS, dtype=jnp.float32) * top_w[..., None], axis=1)
    a = jnp.einsum('td,edf->tef', h, w1)
    u = jnp.einsum('td,edf->tef', h, w3)
    hid = jax.nn.silu(a) * u * gate[:, :, None].astype(h.dtype)
    return jnp.einsum('tef,efd->td', hid, w2)


def _normal(k, shape, std):
    return jax.random.normal(k, shape, jnp.float32) * std


def setup_inputs(seed: int = 0) -> dict:
    key = jax.random.key(seed)
    ks = jax.random.split(key, 40)
    d = D_MODEL
    ha = N_HEADS_A * HEAD_DIM
    hkv = N_KV_HEADS_A * HEAD_DIM
    hc2 = 2 * N_HEADS_C * HEAD_DIM
    hv = N_HEADS_C * DIFF_V_DIM
    return {
        'x': _normal(ks[0], (BATCH, SEQ, d), 1.0),
        'c': _normal(ks[1], (BATCH, d), 1.0),
        'ctx': _normal(ks[2], (BATCH, CTX_LEN, d), 1.0),
        'c_ctx': _normal(ks[3], (d,), 1.0),
        'ada_w': _normal(ks[4], (DEPTH, d, 6 * d), 0.5 * d ** -0.5),
        'ada_b': _normal(ks[5], (DEPTH, 6 * d), 0.02),
        'ln_g': 1.0 + _normal(ks[6], (DEPTH, 2, d), 0.02),
        'ln_b': _normal(ks[7], (DEPTH, 2, d), 0.02),
        'attn_wq': _normal(ks[8], (N_LAYERS_A, d, ha), d ** -0.5),
        'attn_wk': _normal(ks[9], (N_LAYERS_A, d, hkv), d ** -0.5),
        'attn_wv': _normal(ks[10], (N_LAYERS_A, d, hkv), d ** -0.5),
        'attn_wo': _normal(ks[11], (N_LAYERS_A, ha, d), DN_BETA * ha ** -0.5),
        'attn_qn_g': 1.0 + _normal(ks[12], (N_LAYERS_A, HEAD_DIM), 0.02),
        'attn_kn_g': 1.0 + _normal(ks[13], (N_LAYERS_A, HEAD_DIM), 0.02),
        'pool_w': _normal(ks[14], (N_LAYERS_B, N_POOL_GROUPS, POOL_GROUP_DIM, POOL_GROUP_DIM), DN_BETA * POOL_GROUP_DIM ** -0.5),
        'pool_b': _normal(ks[15], (N_LAYERS_B, N_POOL_GROUPS, POOL_GROUP_DIM), 0.02),
        'pool_scale': 1.0 + _normal(ks[16], (N_LAYERS_B, d), 0.02),
        'diff_wq': _normal(ks[17], (N_LAYERS_C, d, hc2), d ** -0.5),
        'diff_wk': _normal(ks[18], (N_LAYERS_C, d, hc2), d ** -0.5),
        'diff_wv': _normal(ks[19], (N_LAYERS_C, d, hv), d ** -0.5),
        'diff_wo': _normal(ks[20], (N_LAYERS_C, hv, d), DN_BETA * hv ** -0.5),
        'diff_lq1': _normal(ks[21], (N_LAYERS_C, HEAD_DIM), 0.1),
        'diff_lk1': _normal(ks[22], (N_LAYERS_C, HEAD_DIM), 0.1),
        'diff_lq2': _normal(ks[23], (N_LAYERS_C, HEAD_DIM), 0.1),
        'diff_lk2': _normal(ks[24], (N_LAYERS_C, HEAD_DIM), 0.1),
        'diff_subln_g': 1.0 + _normal(ks[25], (N_LAYERS_C, DIFF_V_DIM), 0.02),
        'moe_rg_w': _normal(ks[26], (DEPTH, d, N_EXPERT_GROUPS), d ** -0.5),
        'moe_rg_b': _normal(ks[27], (DEPTH, N_EXPERT_GROUPS), 0.01),
        'moe_re_w': _normal(ks[28], (DEPTH, d, N_EXPERTS), d ** -0.5),
        'moe_re_b': _normal(ks[29], (DEPTH, N_EXPERTS), 0.01),
        'moe_w1': _normal(ks[30], (DEPTH, N_EXPERTS, d, D_EXPERT), d ** -0.5),
        'moe_w3': _normal(ks[31], (DEPTH, N_EXPERTS, d, D_EXPERT), d ** -0.5),
        'moe_w2': _normal(ks[32], (DEPTH, N_EXPERTS, D_EXPERT, d), DN_BETA * D_EXPERT ** -0.5),
    }


def reference(x, c, ctx, c_ctx, ada_w, ada_b, ln_g, ln_b, attn_wq, attn_wk, attn_wv, attn_wo, attn_qn_g, attn_kn_g, pool_w, pool_b, pool_scale, diff_wq, diff_wk, diff_wv, diff_wo, diff_lq1, diff_lk1, diff_lq2, diff_lk2, diff_subln_g, moe_rg_w, moe_rg_b, moe_re_w, moe_re_b, moe_w1, moe_w3, moe_w2):
    b, n, d = x.shape
    lc = ctx.shape[1]
    rope_cos, rope_sin = axial_rope_tables(n, x.dtype)
    silu_c = jax.nn.silu(c)
    silu_cc = jax.nn.silu(c_ctx)
    for i in range(DEPTH):
        last = i == DEPTH - 1
        kind, j = i % N_MIXERS, i // N_MIXERS
        sh1, sc1, g1, sh2, sc2, g2 = jnp.split((silu_c @ ada_w[i] + ada_b[i])[:, None, :], 6, axis=-1)
        csh1, csc1, cg1, csh2, csc2, cg2 = jnp.split(silu_cc @ ada_w[i] + ada_b[i], 6, axis=-1)
        h = x * (1 + sc1) + sh1
        hc = ctx * (1 + csc1) + csh1
        if kind == 0:
            y, yc = gqa_mixer(h, hc, rope_cos, rope_sin, attn_wq[j], attn_wk[j], attn_wv[j], attn_wo[j],
                              attn_qn_g[j], attn_kn_g[j], not last)
        elif kind == 1:
            y = pool_mixer(h, pool_w[j], pool_b[j], pool_scale[j])
            yc = None if last else pool_mixer(hc, pool_w[j], pool_b[j], pool_scale[j])
        else:
            lam_init = 0.8 - 0.6 * math.exp(-0.3 * i)
            y, yc = diff_mixer(h, hc, rope_cos, rope_sin, diff_wq[j], diff_wk[j], diff_wv[j], diff_wo[j],
                               diff_lq1[j], diff_lk1[j], diff_lq2[j], diff_lk2[j], diff_subln_g[j],
                               lam_init, not last)
        x = layer_norm(DN_ALPHA * x + g1 * y, ln_g[i, 0], ln_b[i, 0])
        h = x * (1 + sc2) + sh2
        moe_args = (moe_rg_w[i], moe_rg_b[i], moe_re_w[i], moe_re_b[i], moe_w1[i], moe_w3[i], moe_w2[i])
        if last:
            y2 = hier_moe(h.reshape(b * n, d), *moe_args).reshape(b, n, d)
        else:
            ctx = layer_norm(DN_ALPHA * ctx + cg1 * yc, ln_g[i, 0], ln_b[i, 0])
            hc = ctx * (1 + csc2) + csh2
            tokens = jnp.concatenate([hc, h], axis=1).reshape(b * (lc + n), d)
            out = hier_moe(tokens, *moe_args).reshape(b, lc + n, d)
            ctx = layer_norm(DN_ALPHA * ctx + cg2 * out[:, :lc], ln_g[i, 1], ln_b[i, 1])
            y2 = out[:, lc:]
        x = layer_norm(DN_ALPHA * x + g2 * y2, ln_g[i, 1], ln_b[i, 1])
    return x
```

```python
import functools
import math

import jax
import jax.numpy as jnp
from jax import lax
from jax.experimental import pallas as pl
from jax.experimental.pallas import tpu as pltpu

HEAD_DIM = 128
GRID_W = 64
ROPE_THETA = 10000.0
POOL_WINDOWS = (2, 4, 8, 16)
POOL_HALO = 8
N_EXPERT_GROUPS = 4
EXPERTS_PER_GROUP = 4
N_EXPERTS = N_EXPERT_GROUPS * EXPERTS_PER_GROUP
N_MIXERS = 3
GQA_GROUP = 4
LN_EPS = 1e-6
RMS_EPS = 1e-6
LANES = 128
SUBLANES = 8
VMEM_LIMIT_BYTES = 56 * 1024 * 1024

F32 = jnp.float32
BF16 = jnp.bfloat16


def _params(*sem):
    return pltpu.CompilerParams(dimension_semantics=sem, vmem_limit_bytes=VMEM_LIMIT_BYTES)


def _tile(n, target, mult=SUBLANES):
    best = None
    for t in range(mult, min(n, target) + 1, mult):
        if n % t == 0:
            best = t
    assert best is not None, (n, target, mult)
    return best


def _sigmoid(v):
    return 1.0 / (1.0 + jnp.exp(-v))


def _ada_kernel(c_ref, w_ref, b_ref, o_ref, acc_ref, *, tk, tn):
    k = pl.program_id(2)

    @pl.when(k == 0)
    def _():
        acc_ref[...] = jnp.zeros_like(acc_ref)

    cv = c_ref[...]
    sv = cv * _sigmoid(cv)
    s0, s1 = sv[0], sv[1]
    for j in range(tn // LANES):
        cols = slice(j * LANES, (j + 1) * LANES)
        wj = w_ref[0, :, cols]
        acc_ref[0, :, cols] += (wj * s0).reshape(tk // SUBLANES, SUBLANES, LANES).sum(axis=0)
        acc_ref[1, :, cols] += (wj * s1).reshape(tk // SUBLANES, SUBLANES, LANES).sum(axis=0)

    @pl.when(k == pl.num_programs(2) - 1)
    def _():
        o_ref[0] = acc_ref[...].sum(axis=1) + b_ref[0]


def _ada_all(cvec, ada_w, ada_b):
    n_layers, d, n_out = ada_w.shape
    tk = _tile(d, 1024)
    tn = _tile(n_out, 2048, LANES)
    c_rep = jnp.broadcast_to(cvec[:, :, None], (2, d, LANES))
    return pl.pallas_call(
        functools.partial(_ada_kernel, tk=tk, tn=tn),
        grid=(n_layers, n_out // tn, d // tk),
        in_specs=[
            pl.BlockSpec((2, tk, LANES), lambda l, j, k: (0, k, 0)),
            pl.BlockSpec((1, tk, tn), lambda l, j, k: (l, k, j)),
            pl.BlockSpec((1, 1, tn), lambda l, j, k: (l, 0, j)),
        ],
        out_specs=pl.BlockSpec((1, 2, tn), lambda l, j, k: (l, 0, j)),
        out_shape=jax.ShapeDtypeStruct((n_layers, 2, n_out), F32),
        scratch_shapes=[pltpu.VMEM((2, SUBLANES, tn), F32)],
        compiler_params=_params("parallel", "parallel", "arbitrary"),
        name="ada_mod",
    )(c_rep, ada_w, ada_b.reshape(n_layers, 1, n_out))


def _route_rows(lg):
    ng, ne = N_EXPERT_GROUPS, EXPERTS_PER_GROUP
    g = [lg[r:r + 1, :] for r in range(ng)]
    gmax = functools.reduce(jnp.maximum, g)
    gidx = jnp.full(gmax.shape, ng - 1, jnp.int32)
    for r in range(ng - 2, -1, -1):
        gidx = jnp.where(g[r] == gmax, r, gidx)
    p_g = 1.0 / functools.reduce(jnp.add, [jnp.exp(gr - gmax) for gr in g])
    sel = []
    for j in range(ne):
        v = lg[ng + (ng - 1) * ne + j:ng + (ng - 1) * ne + j + 1, :]
        for r in range(ng - 2, -1, -1):
            v = jnp.where(gidx == r, lg[ng + r * ne + j:ng + r * ne + j + 1, :], v)
        sel.append(v)
    v1 = functools.reduce(jnp.maximum, sel)
    i1 = jnp.full(v1.shape, ne - 1, jnp.int32)
    for j in range(ne - 2, -1, -1):
        i1 = jnp.where(sel[j] == v1, j, i1)
    rest = [jnp.where(i1 == j, -jnp.inf, sel[j]) for j in range(ne)]
    v2 = functools.reduce(jnp.maximum, rest)
    i2 = jnp.full(v2.shape, ne - 1, jnp.int32)
    for j in range(ne - 2, -1, -1):
        i2 = jnp.where(rest[j] == v2, j, i2)
    t = jnp.exp(v2 - v1)
    w1 = p_g / (1.0 + t)
    w2 = p_g * t / (1.0 + t)
    ids = jnp.concatenate([gidx * ne + i1, gidx * ne + i2], axis=0)
    wts = jnp.concatenate([w1, w2], axis=0)
    return ids, wts


def _ln_mod_kernel(*refs, n_y, do_ln, out_h, do_route, alpha):
    refs = list(refs)
    x_ref = refs.pop(0)
    y_refs = [refs.pop(0) for _ in range(n_y)]
    if do_ln:
        gate_ref, lng_ref, lnb_ref = refs.pop(0), refs.pop(0), refs.pop(0)
    if out_h is not None:
        sc_ref, sh_ref = refs.pop(0), refs.pop(0)
    if do_route:
        whi_ref, wlo_ref, rb_ref = refs.pop(0), refs.pop(0), refs.pop(0)
    xo_ref = refs.pop(0) if do_ln else None
    h_ref = refs.pop(0) if out_h is not None else None
    if do_route:
        ids_ref, wts_ref = refs.pop(0), refs.pop(0)
    assert not refs

    x = x_ref[...]
    if do_ln:
        y = y_refs[0][0] if n_y == 2 else y_refs[0][...]
        if n_y == 2:
            y = y + y_refs[1][0]
        z = alpha * x + gate_ref[0] * y
        mu = jnp.mean(z, axis=-1, keepdims=True)
        zc = z - mu
        var = jnp.mean(zc * zc, axis=-1, keepdims=True)
        x = zc * lax.rsqrt(var + LN_EPS) * lng_ref[...] + lnb_ref[...]
        xo_ref[...] = x
    if out_h is not None:
        h = x * (1.0 + sc_ref[0]) + sh_ref[0]
        h_ref[...] = h.astype(out_h)
    if do_route:
        h_hi = h.astype(BF16)
        h_lo = (h - h_hi.astype(F32)).astype(BF16)
        w_hi = whi_ref[...]
        lg = (jnp.dot(h_hi, w_hi, preferred_element_type=F32)
              + jnp.dot(h_lo, w_hi, preferred_element_type=F32)
              + jnp.dot(h_hi, wlo_ref[...], preferred_element_type=F32)) + rb_ref[...]
        ids, wts = _route_rows(lg.T)
        pad = SUBLANES - ids.shape[0]
        ids_ref[...] = jnp.concatenate([ids, jnp.zeros((pad, ids.shape[1]), jnp.int32)], axis=0)
        wts_ref[...] = jnp.concatenate([wts, jnp.zeros((pad, wts.shape[1]), F32)], axis=0)


def _ln_mod(x, ys, gate, ln_g, ln_b, sc, sh, route_w, *, lc, alpha, out_h, row_offset=0):
    t_all, d = x.shape
    do_ln = len(ys) > 0
    do_route = route_w is not None
    tm = _tile(math.gcd(lc, t_all - lc), 128)
    assert row_offset % tm == 0
    off = row_offset // tm
    t_out = t_all - row_offset
    nct = lc // tm

    row = lambda i: (i + off, 0)
    stream = lambda i: (jnp.where(i + off < nct, 0, 1), 0, 0)
    vec = pl.BlockSpec((1, d), lambda i: (0, 0))
    svec = pl.BlockSpec((1, 1, d), stream)

    args, in_specs = [x], [pl.BlockSpec((tm, d), row)]
    n_y = 0
    for y in ys:
        if y.ndim == 3:
            n_y = 2
            args += [y, y]
            in_specs += [pl.BlockSpec((1, tm, d), lambda i: (0, i + off, 0)),
                         pl.BlockSpec((1, tm, d), lambda i: (1, i + off, 0))]
        else:
            n_y = 1
            args.append(y)
            in_specs.append(pl.BlockSpec((tm, d), row))
    if do_ln:
        args += [gate, ln_g.reshape(1, d), ln_b.reshape(1, d)]
        in_specs += [svec, vec, vec]
    if out_h is not None:
        args += [sc, sh]
        in_specs += [svec, svec]
    if do_route:
        args += list(route_w)
        in_specs += [pl.BlockSpec((d, LANES), lambda i: (0, 0)),
                     pl.BlockSpec((d, LANES), lambda i: (0, 0)),
                     pl.BlockSpec((1, LANES), lambda i: (0, 0))]
    out_shape, out_specs = [], []
    if do_ln:
        out_shape.append(jax.ShapeDtypeStruct((t_out, d), F32))
        out_specs.append(pl.BlockSpec((tm, d), lambda i: (i, 0)))
    if out_h is not None:
        out_shape.append(jax.ShapeDtypeStruct((t_out, d), out_h))
        out_specs.append(pl.BlockSpec((tm, d), lambda i: (i, 0)))
    if do_route:
        out_shape += [jax.ShapeDtypeStruct((SUBLANES, t_out), jnp.int32),
                      jax.ShapeDtypeStruct((SUBLANES, t_out), F32)]
        out_specs += [pl.BlockSpec((SUBLANES, tm), lambda i: (0, i))] * 2
    return pl.pallas_call(
        functools.partial(_ln_mod_kernel, n_y=n_y, do_ln=do_ln, out_h=out_h,
                          do_route=do_route, alpha=alpha),
        grid=(t_out // tm,),
        in_specs=in_specs,
        out_specs=out_specs,
        out_shape=out_shape,
        compiler_params=_params("parallel"),
        name="ln_mod",
    )(*args)


def _proj_kernel(*refs, norm, rope, scale, tn):
    refs = list(refs)
    a_ref, w_ref = refs.pop(0), refs.pop(0)
    g_ref = refs.pop(0) if norm else None
    if rope:
        cos_ref, sin_ref = refs.pop(0), refs.pop(0)
    o_ref = refs.pop(0)
    acc = jnp.dot(a_ref[...], w_ref[...], preferred_element_type=F32)
    if not (norm or rope):
        if scale != 1.0:
            acc = acc * scale
        o_ref[...] = acc.astype(o_ref.dtype)
        return
    if rope:
        cos, sin = cos_ref[...], sin_ref[...]
        lane = lax.broadcasted_iota(jnp.int32, cos.shape, 1)
        first_half = (lane % (HEAD_DIM // 2)) < (HEAD_DIM // 4)
    for hh in range(tn // HEAD_DIM):
        cols = slice(hh * HEAD_DIM, (hh + 1) * HEAD_DIM)
        t = acc[:, cols]
        if norm:
            t = t * lax.rsqrt(jnp.mean(t * t, axis=-1, keepdims=True) + RMS_EPS) * g_ref[...]
        if rope:
            up = pltpu.roll(t, HEAD_DIM - HEAD_DIM // 4, 1)
            dn = pltpu.roll(t, HEAD_DIM // 4, 1)
            t = t * cos + jnp.where(first_half, up, dn) * sin
        if scale != 1.0:
            t = t * scale
        o_ref[:, cols] = t.astype(o_ref.dtype)


def _proj(a, w, *, out_dtype, norm_g=None, rope=None, scale=1.0, tm_target=768, tn_target=1024):
    t_all, kdim = a.shape
    n = w.shape[1]
    tm = _tile(t_all, tm_target)
    tn = _tile(n, tn_target, LANES)
    args = [a, w]
    in_specs = [pl.BlockSpec((tm, kdim), lambda i, j: (i, 0)),
                pl.BlockSpec((kdim, tn), lambda i, j: (0, j))]
    if norm_g is not None:
        args.append(norm_g.reshape(1, HEAD_DIM))
        in_specs.append(pl.BlockSpec((1, HEAD_DIM), lambda i, j: (0, 0)))
    if rope is not None:
        args += list(rope)
        in_specs += [pl.BlockSpec((tm, HEAD_DIM), lambda i, j: (i, 0))] * 2
    return pl.pallas_call(
        functools.partial(_proj_kernel, norm=norm_g is not None, rope=rope is not None,
                          scale=scale, tn=tn),
        grid=(t_all // tm, n // tn),
        in_specs=in_specs,
        out_specs=pl.BlockSpec((tm, tn), lambda i, j: (i, j)),
        out_shape=jax.ShapeDtypeStruct((t_all, n), out_dtype),
        compiler_params=_params("parallel", "parallel"),
        name="proj",
    )(*args)


def _rope_tables(n, lc):
    n_freq = HEAD_DIM // 4
    pos = jnp.arange(n, dtype=jnp.int32)
    row = (pos // GRID_W).astype(F32)
    col = (pos % GRID_W).astype(F32)
    inv_freq = ROPE_THETA ** (-jnp.arange(n_freq, dtype=F32) / n_freq)
    ar, ac = row[:, None] * inv_freq, col[:, None] * inv_freq
    cos = jnp.concatenate([jnp.cos(ar), jnp.cos(ar), jnp.cos(ac), jnp.cos(ac)], axis=1)
    sin = jnp.concatenate([-jnp.sin(ar), jnp.sin(ar), -jnp.sin(ac), jnp.sin(ac)], axis=1)
    cos = jnp.concatenate([jnp.ones((lc, HEAD_DIM), F32), cos], axis=0)
    sin = jnp.concatenate([jnp.zeros((lc, HEAD_DIM), F32), sin], axis=0)
    return cos, sin


def _softmax_step(q, kc, vc, m, l, acc):
    s = lax.dot_general(q, kc, (((1,), (1,)), ((), ())), preferred_element_type=F32)
    m_new = jnp.maximum(m, jnp.max(s, axis=-1, keepdims=True))
    a = jnp.exp(m - m_new)
    p = jnp.exp(s - m_new)
    l_new = a * l + jnp.sum(p, axis=-1, keepdims=True)
    acc_new = a * acc + jnp.dot(p.astype(vc.dtype), vc, preferred_element_type=F32)
    return m_new, l_new, acc_new


def _sweep(q, k_ref, v_ref, kcols, n_lat_chunks, *, lc, tk, dv):
    rows = q.shape[0]
    init = (jnp.full((rows, 1), -jnp.inf, F32), jnp.zeros((rows, 1), F32), jnp.zeros((rows, dv), F32))
    carry = _softmax_step(q, k_ref[0:lc, kcols], v_ref[0:lc, :], *init)

    def body(j, c):
        start = pl.multiple_of(lc + j * tk, math.gcd(lc, tk))
        return _softmax_step(q, k_ref[pl.ds(start, tk), kcols], v_ref[pl.ds(start, tk), :], *c)

    m, l, acc = lax.fori_loop(0, n_lat_chunks, body, carry)
    return acc / l


def _gqa_kernel(q_ref, k_ref, v_ref, o_ref, *, lc, tq, tk, n_lat):
    i = pl.program_id(1)
    q = jnp.concatenate([q_ref[:, h * HEAD_DIM:(h + 1) * HEAD_DIM] for h in range(GQA_GROUP)], axis=0)
    n_chunks = jnp.where(i * tq < lc, 0, n_lat // tk)
    o = _sweep(q, k_ref, v_ref, slice(0, HEAD_DIM), n_chunks, lc=lc, tk=tk, dv=HEAD_DIM)
    for h in range(GQA_GROUP):
        o_ref[:, h * HEAD_DIM:(h + 1) * HEAD_DIM] = o[h * tq:(h + 1) * tq].astype(o_ref.dtype)


def _gqa_attention(q, k, v, *, lc, tq_target=256, tk_target=512):
    t_all, dq = q.shape
    n_kv = k.shape[1] // HEAD_DIM
    n_lat = t_all - lc
    tq = _tile(math.gcd(lc, n_lat), tq_target)
    tk = _tile(n_lat, tk_target)
    gw = GQA_GROUP * HEAD_DIM
    return pl.pallas_call(
        functools.partial(_gqa_kernel, lc=lc, tq=tq, tk=tk, n_lat=n_lat),
        grid=(n_kv, t_all // tq),
        in_specs=[pl.BlockSpec((tq, gw), lambda g, i: (i, g)),
                  pl.BlockSpec((t_all, HEAD_DIM), lambda g, i: (0, g)),
                  pl.BlockSpec((t_all, HEAD_DIM), lambda g, i: (0, g))],
        out_specs=pl.BlockSpec((tq, gw), lambda g, i: (i, g)),
        out_shape=jax.ShapeDtypeStruct((t_all, dq), BF16),
        compiler_params=_params("parallel", "parallel"),
        name="gqa_attention",
    )(q, k, v)


def _diff_kernel(q_ref, k_ref, v_ref, lq1_ref, lk1_ref, lq2_ref, lk2_ref, g_ref, o_ref,
                 *, lc, tq, tk, n_lat, lam_init):
    i = pl.program_id(1)
    n_chunks = jnp.where(i * tq < lc, 0, n_lat // tk)
    dv = 2 * HEAD_DIM
    o1 = _sweep(q_ref[:, 0:HEAD_DIM], k_ref, v_ref, slice(0, HEAD_DIM), n_chunks, lc=lc, tk=tk, dv=dv)
    o2 = _sweep(q_ref[:, HEAD_DIM:dv], k_ref, v_ref, slice(HEAD_DIM, dv), n_chunks, lc=lc, tk=tk, dv=dv)
    lam = (jnp.exp(jnp.sum(lq1_ref[...] * lk1_ref[...], axis=-1, keepdims=True))
           - jnp.exp(jnp.sum(lq2_ref[...] * lk2_ref[...], axis=-1, keepdims=True)) + lam_init)
    o = o1 - lam * o2
    o = o * lax.rsqrt(jnp.mean(o * o, axis=-1, keepdims=True) + RMS_EPS) * g_ref[...]
    o_ref[...] = (o * (1.0 - lam_init)).astype(o_ref.dtype)


def _diff_attention(q, k, v, lq1, lk1, lq2, lk2, subln_g, *, lc, lam_init, tq_target=256, tk_target=512):
    t_all, dq = q.shape
    dv = 2 * HEAD_DIM
    n_heads = dq // dv
    n_lat = t_all - lc
    tq = _tile(math.gcd(lc, n_lat), tq_target)
    tk = _tile(n_lat, tk_target)
    vec = lambda n: pl.BlockSpec((1, n), lambda h, i: (0, 0))
    return pl.pallas_call(
        functools.partial(_diff_kernel, lc=lc, tq=tq, tk=tk, n_lat=n_lat, lam_init=lam_init),
        grid=(n_heads, t_all // tq),
        in_specs=[pl.BlockSpec((tq, dv), lambda h, i: (i, h)),
                  pl.BlockSpec((t_all, dv), lambda h, i: (0, h)),
                  pl.BlockSpec((t_all, dv), lambda h, i: (0, h)),
                  vec(HEAD_DIM), vec(HEAD_DIM), vec(HEAD_DIM), vec(HEAD_DIM), vec(dv)],
        out_specs=pl.BlockSpec((tq, dv), lambda h, i: (i, h)),
        out_shape=jax.ShapeDtypeStruct((t_all, v.shape[1]), BF16),
        compiler_params=_params("parallel", "parallel"),
        name="diff_attention",
    )(q, k, v, lq1.reshape(1, -1), lk1.reshape(1, -1), lq2.reshape(1, -1), lk2.reshape(1, -1),
      subln_g.reshape(1, -1))


def _pool_kernel(x_ref, xp_ref, xn_ref, sc_ref, sh_ref, w_ref, b_ref, ps_ref, o_ref, buf_ref,
                 *, lc, n_lat, tm):
    i = pl.program_id(0)
    nct = lc // tm
    nt = (lc + n_lat) // tm
    sc, sh = sc_ref[0], sh_ref[0]
    first = jnp.logical_or(i == 0, i == nct)
    last = jnp.logical_or(i == nct - 1, i == nt - 1)
    cur = x_ref[...] * (1.0 + sc) + sh
    buf_ref[0:POOL_HALO, :] = jnp.where(first, 0.0, xp_ref[...] * (1.0 + sc) + sh)
    buf_ref[POOL_HALO:POOL_HALO + tm, :] = cur
    buf_ref[POOL_HALO + tm:2 * POOL_HALO + tm, :] = jnp.where(last, 0.0, xn_ref[...] * (1.0 + sc) + sh)

    in_ctx = i < nct
    t_loc = (lax.broadcasted_iota(jnp.int32, (tm, 1), 0)
             + (i - jnp.where(in_ctx, 0, nct)) * tm)
    n_s = jnp.where(in_ctx, lc, n_lat)
    cdim = w_ref.shape[1]
    for g, win in enumerate(POOL_WINDOWS):
        cols = slice(g * cdim, (g + 1) * cdim)
        half = win // 2
        acc = buf_ref[POOL_HALO - half:POOL_HALO - half + tm, cols]
        for k in range(-half + 1, half):
            acc = acc + buf_ref[POOL_HALO + k:POOL_HALO + k + tm, cols]
        cnt = jnp.minimum(t_loc + half - 1, n_s - 1) - jnp.maximum(t_loc - half, 0) + 1
        y = acc / cnt.astype(F32) - cur[:, cols]
        out = jnp.dot(y.astype(BF16), w_ref[g], preferred_element_type=F32) + b_ref[g]
        o_ref[:, cols] = out * ps_ref[:, cols]


def _pool_mixer(x, sc, sh, pool_w, pool_b, pool_scale, *, lc):
    t_all, d = x.shape
    n_lat = t_all - lc
    ng, cdim, _ = pool_w.shape
    tm = _tile(math.gcd(lc, n_lat), 128)
    hb = tm // POOL_HALO
    n_hb = t_all // POOL_HALO
    stream = lambda i: (jnp.where(i < lc // tm, 0, 1), 0, 0)
    return pl.pallas_call(
        functools.partial(_pool_kernel, lc=lc, n_lat=n_lat, tm=tm),
        grid=(t_all // tm,),
        in_specs=[pl.BlockSpec((tm, d), lambda i: (i, 0)),
                  pl.BlockSpec((POOL_HALO, d), lambda i: (jnp.maximum(i * hb - 1, 0), 0)),
                  pl.BlockSpec((POOL_HALO, d), lambda i: (jnp.minimum((i + 1) * hb, n_hb - 1), 0)),
                  pl.BlockSpec((1, 1, d), stream),
                  pl.BlockSpec((1, 1, d), stream),
                  pl.BlockSpec((ng, cdim, cdim), lambda i: (0, 0, 0)),
                  pl.BlockSpec((ng, 1, cdim), lambda i: (0, 0, 0)),
                  pl.BlockSpec((1, d), lambda i: (0, 0))],
        out_specs=pl.BlockSpec((tm, d), lambda i: (i, 0)),
        out_shape=jax.ShapeDtypeStruct((t_all, d), F32),
        scratch_shapes=[pltpu.VMEM((tm + 2 * POOL_HALO, d), F32)],
        compiler_params=_params("parallel"),
        name="pool_mixer",
    )(x, x, x, sc, sh, pool_w.astype(BF16), pool_b.reshape(ng, 1, cdim), pool_scale.reshape(1, d))


def _moe_kernel(texp_ref, rtok_ref, rdst_ref, nact_ref,
                h_hbm, gate_ref, w1_ref, w3_ref, w2_ref, y_hbm,
                xbuf, ybuf, sem_in, sem_out, *, tm):
    i = pl.program_id(0)

    @pl.when(i < nact_ref[0])
    def _():
        base = i * tm

        def gather(k):
            tok = rtok_ref[base + k]
            return pltpu.make_async_copy(h_hbm.at[pl.ds(tok, 1)], xbuf.at[pl.ds(k, 1)], sem_in)

        def scatter(k):
            dst = jnp.maximum(rdst_ref[base + k], 0)
            return pltpu.make_async_copy(ybuf.at[pl.ds(k, 1)], y_hbm.at[pl.ds(dst, 1)], sem_out)

        def each_row(fn, only_valid):
            def body(k, carry):
                if only_valid:
                    pl.when(rdst_ref[base + k] >= 0)(lambda: fn(k))
                else:
                    fn(k)
                return carry
            lax.fori_loop(0, tm, body, 0)

        each_row(lambda k: gather(k).start(), False)
        each_row(lambda k: gather(k).wait(), False)
        x = xbuf[...].astype(BF16)
        a = jnp.dot(x, w1_ref[0], preferred_element_type=F32)
        u = jnp.dot(x, w3_ref[0], preferred_element_type=F32)
        hid = a * _sigmoid(a) * u * gate_ref[...]
        ybuf[...] = jnp.dot(hid.astype(BF16), w2_ref[0], preferred_element_type=F32)
        each_row(lambda k: scatter(k).start(), True)
        each_row(lambda k: scatter(k).wait(), True)


def _moe(h, ids, wts, w1, w3, w2, *, tm_target=256):
    t_all, d = h.shape
    n_exp, _, f = w1.shape
    tm = _tile(t_all, tm_target)
    n_pairs = 2 * t_all
    n_tiles = n_pairs // tm + n_exp
    n_rows = n_tiles * tm

    e_flat = ids.reshape(n_pairs)
    w_flat = wts.reshape(n_pairs)
    order = jnp.argsort(e_flat, stable=True).astype(jnp.int32)
    e_sorted = e_flat[order]
    counts = jnp.sum(e_flat[None, :] == jnp.arange(n_exp, dtype=jnp.int32)[:, None], axis=1, dtype=jnp.int32)
    padded = (counts + tm - 1) // tm * tm
    p_end = jnp.cumsum(padded)
    c_start = jnp.cumsum(counts) - counts
    pos = (p_end - padded)[e_sorted] + jnp.arange(n_pairs, dtype=jnp.int32) - c_start[e_sorted]
    row_pair = jnp.full((n_rows,), -1, jnp.int32).at[pos].set(order)
    valid = row_pair >= 0
    safe = jnp.maximum(row_pair, 0)
    row_tok = jnp.where(valid, safe % t_all, 0)
    row_gate = jnp.where(valid, w_flat[safe], 0.0).reshape(n_rows, 1)
    n_act = (p_end[-1] // tm).astype(jnp.int32)
    tile_exp = jnp.searchsorted(p_end, jnp.arange(n_tiles, dtype=jnp.int32) * tm, side="right")
    tile_exp = jnp.minimum(tile_exp, n_exp - 1).astype(jnp.int32)
    tile_exp = jnp.where(jnp.arange(n_tiles) < n_act, tile_exp, tile_exp[jnp.maximum(n_act - 1, 0)])

    wspec = lambda shape: pl.BlockSpec(shape, lambda i, te, rt, rd, na: (te[i], 0, 0))
    y = pl.pallas_call(
        functools.partial(_moe_kernel, tm=tm),
        grid_spec=pltpu.PrefetchScalarGridSpec(
            num_scalar_prefetch=4,
            grid=(n_tiles,),
            in_specs=[pl.BlockSpec(memory_space=pl.ANY),
                      pl.BlockSpec((tm, 1), lambda i, te, rt, rd, na: (i, 0)),
                      wspec((1, d, f)), wspec((1, d, f)), wspec((1, f, d))],
            out_specs=pl.BlockSpec(memory_space=pl.ANY),
            scratch_shapes=[pltpu.VMEM((tm, d), F32), pltpu.VMEM((tm, d), F32),
                            pltpu.SemaphoreType.DMA, pltpu.SemaphoreType.DMA],
        ),
        out_shape=jax.ShapeDtypeStruct((n_pairs, d), F32),
        compiler_params=_params("arbitrary"),
        name="moe_experts",
    )(tile_exp, row_tok, row_pair, n_act.reshape(1), h, row_gate,
      w1.astype(BF16), w3.astype(BF16), w2.astype(BF16))
    return y.reshape(2, t_all, d)


def _router_weights(rg_w, rg_b, re_w, re_b):
    d = rg_w.shape[0]
    n = rg_w.shape[1] + re_w.shape[1]
    w = jnp.concatenate([rg_w, re_w, jnp.zeros((d, LANES - n), F32)], axis=1)
    b = jnp.concatenate([rg_b, re_b, jnp.zeros((LANES - n,), F32)]).reshape(1, LANES)
    w_hi = w.astype(BF16)
    w_lo = (w - w_hi.astype(F32)).astype(BF16)
    return w_hi, w_lo, b


def kernel(x, c, ctx, c_ctx, ada_w, ada_b, ln_g, ln_b, attn_wq, attn_wk, attn_wv, attn_wo, attn_qn_g, attn_kn_g, pool_w, pool_b, pool_scale, diff_wq, diff_wk, diff_wv, diff_wo, diff_lq1, diff_lk1, diff_lq2, diff_lk2, diff_subln_g, moe_rg_w, moe_rg_b, moe_re_w, moe_re_b, moe_w1, moe_w3, moe_w2):
    b, n, d = x.shape
    assert b == 1 and c.shape[0] == 1 and ctx.shape[0] == 1
    lc = ctx.shape[1]
    depth = ada_w.shape[0]
    alpha = (2 * depth) ** 0.25
    qk_scale = 1.0 / math.sqrt(HEAD_DIM)

    s = jnp.concatenate([ctx[0], x[0]], axis=0)
    mods = _ada_all(jnp.concatenate([c_ctx[None, :], c], axis=0), ada_w, ada_b)
    mod = lambda i, k: mods[i, :, k * d:(k + 1) * d].reshape(2, 1, d)
    rope = _rope_tables(n, lc)
    bf = lambda w: w.astype(BF16)

    h1 = None
    for i in range(depth):
        last = i == depth - 1
        kind, j = i % N_MIXERS, i // N_MIXERS
        if kind != 1 and h1 is None:
            (h1,) = _ln_mod(s, (), None, None, None, mod(i, 1), mod(i, 0), None,
                            lc=lc, alpha=alpha, out_h=BF16)
        if kind == 0:
            q = _proj(h1, bf(attn_wq[j]), out_dtype=BF16, norm_g=attn_qn_g[j], rope=rope, scale=qk_scale)
            k = _proj(h1, bf(attn_wk[j]), out_dtype=BF16, norm_g=attn_kn_g[j], rope=rope)
            v = _proj(h1, bf(attn_wv[j]), out_dtype=BF16)
            o = _gqa_attention(q, k, v, lc=lc)
            y = _proj(o, bf(attn_wo[j]), out_dtype=F32)
        elif kind == 1:
            y = _pool_mixer(s, mod(i, 1), mod(i, 0), pool_w[j], pool_b[j], pool_scale[j], lc=lc)
        else:
            lam_init = 0.8 - 0.6 * math.exp(-0.3 * i)
            q = _proj(h1, bf(diff_wq[j]), out_dtype=BF16, rope=rope, scale=qk_scale)
            k = _proj(h1, bf(diff_wk[j]), out_dtype=BF16, rope=rope)
            v = _proj(h1, bf(diff_wv[j]), out_dtype=BF16)
            o = _diff_attention(q, k, v, diff_lq1[j], diff_lk1[j], diff_lq2[j], diff_lk2[j],
                                diff_subln_g[j], lc=lc, lam_init=lam_init)
            y = _proj(o, bf(diff_wo[j]), out_dtype=F32)
        h1 = None
        route_w = _router_weights(moe_rg_w[i], moe_rg_b[i], moe_re_w[i], moe_re_b[i])
        s, h2, ids, wts = _ln_mod(s, (y,), mod(i, 2), ln_g[i, 0], ln_b[i, 0], mod(i, 4), mod(i, 3),
                                  route_w, lc=lc, alpha=alpha, out_h=F32)
        y2 = _moe(h2, ids[:2], wts[:2], moe_w1[i], moe_w3[i], moe_w2[i])
        if last:
            (s,) = _ln_mod(s, (y2,), mod(i, 5), ln_g[i, 1], ln_b[i, 1], None, None, None,
                           lc=lc, alpha=alpha, out_h=None, row_offset=lc)
        elif (i + 1) % N_MIXERS == 1:
            (s,) = _ln_mod(s, (y2,), mod(i, 5), ln_g[i, 1], ln_b[i, 1], None, None, None,
                           lc=lc, alpha=alpha, out_h=None)
        else:
            s, h1 = _ln_mod(s, (y2,), mod(i, 5), ln_g[i, 1], ln_b[i, 1], mod(i + 1, 1), mod(i + 1, 0),
                            None, lc=lc, alpha=alpha, out_h=BF16)
    return s[None]
```

```python
import functools
import math

import jax
import jax.numpy as jnp
from jax import lax
from jax.experimental import pallas as pl
from jax.experimental.pallas import tpu as pltpu

HEAD_DIM = 128
GRID_W = 64
ROPE_THETA = 10000.0
POOL_WINDOWS = (2, 4, 8, 16)
POOL_HALO = 8
N_EXPERT_GROUPS = 4
EXPERTS_PER_GROUP = 4
N_EXPERTS = N_EXPERT_GROUPS * EXPERTS_PER_GROUP
N_MIXERS = 3
GQA_GROUP = 4
LN_EPS = 1e-6
RMS_EPS = 1e-6
LANES = 128
SUBLANES = 8
VMEM_LIMIT_BYTES = 56 * 1024 * 1024

F32 = jnp.float32
BF16 = jnp.bfloat16


def _params(*sem):
    return pltpu.CompilerParams(dimension_semantics=sem, vmem_limit_bytes=VMEM_LIMIT_BYTES)


def _tile(n, target, mult=SUBLANES):
    best = None
    for t in range(mult, min(n, target) + 1, mult):
        if n % t == 0:
            best = t
    assert best is not None, (n, target, mult)
    return best


def _sigmoid(v):
    return 1.0 / (1.0 + jnp.exp(-v))


def _ada_kernel(c_ref, w_ref, b_ref, o_ref, acc_ref, *, tk, tn):
    k = pl.program_id(2)

    @pl.when(k == 0)
    def _():
        acc_ref[...] = jnp.zeros_like(acc_ref)

    cv = c_ref[...]
    sv = cv * _sigmoid(cv)
    s0, s1 = sv[0], sv[1]
    for j in range(tn // LANES):
        cols = slice(j * LANES, (j + 1) * LANES)
        wj = w_ref[0, :, cols]
        acc_ref[0, :, cols] += (wj * s0).reshape(tk // SUBLANES, SUBLANES, LANES).sum(axis=0)
        acc_ref[1, :, cols] += (wj * s1).reshape(tk // SUBLANES, SUBLANES, LANES).sum(axis=0)

    @pl.when(k == pl.num_programs(2) - 1)
    def _():
        o_ref[0] = acc_ref[...].sum(axis=1) + b_ref[0]


def _ada_all(cvec, ada_w, ada_b):
    n_layers, d, n_out = ada_w.shape
    tk = _tile(d, 1024)
    tn = _tile(n_out, 2048, LANES)
    c_rep = jnp.broadcast_to(cvec[:, :, None], (2, d, LANES))
    return pl.pallas_call(
        functools.partial(_ada_kernel, tk=tk, tn=tn),
        grid=(n_layers, n_out // tn, d // tk),
        in_specs=[
            pl.BlockSpec((2, tk, LANES), lambda l, j, k: (0, k, 0)),
            pl.BlockSpec((1, tk, tn), lambda l, j, k: (l, k, j)),
            pl.BlockSpec((1, 1, tn), lambda l, j, k: (l, 0, j)),
        ],
        out_specs=pl.BlockSpec((1, 2, tn), lambda l, j, k: (l, 0, j)),
        out_shape=jax.ShapeDtypeStruct((n_layers, 2, n_out), F32),
        scratch_shapes=[pltpu.VMEM((2, SUBLANES, tn), F32)],
        compiler_params=_params("parallel", "parallel", "arbitrary"),
        name="ada_mod",
    )(c_rep, ada_w, ada_b.reshape(n_layers, 1, n_out))


def _route_rows(lg):
    ng, ne = N_EXPERT_GROUPS, EXPERTS_PER_GROUP
    g = [lg[r:r + 1, :] for r in range(ng)]
    gmax = functools.reduce(jnp.maximum, g)
    gidx = jnp.full(gmax.shape, ng - 1, jnp.int32)
    for r in range(ng - 2, -1, -1):
        gidx = jnp.where(g[r] == gmax, r, gidx)
    p_g = 1.0 / functools.reduce(jnp.add, [jnp.exp(gr - gmax) for gr in g])
    sel = []
    for j in range(ne):
        v = lg[ng + (ng - 1) * ne + j:ng + (ng - 1) * ne + j + 1, :]
        for r in range(ng - 2, -1, -1):
            v = jnp.where(gidx == r, lg[ng + r * ne + j:ng + r * ne + j + 1, :], v)
        sel.append(v)
    v1 = functools.reduce(jnp.maximum, sel)
    i1 = jnp.full(v1.shape, ne - 1, jnp.int32)
    for j in range(ne - 2, -1, -1):
        i1 = jnp.where(sel[j] == v1, j, i1)
    rest = [jnp.where(i1 == j, -jnp.inf, sel[j]) for j in range(ne)]
    v2 = functools.reduce(jnp.maximum, rest)
    i2 = jnp.full(v2.shape, ne - 1, jnp.int32)
    for j in range(ne - 2, -1, -1):
        i2 = jnp.where(rest[j] == v2, j, i2)
    t = jnp.exp(v2 - v1)
    w1 = p_g / (1.0 + t)
    w2 = p_g * t / (1.0 + t)
    ids = jnp.concatenate([gidx * ne + i1, gidx * ne + i2], axis=0)
    wts = jnp.concatenate([w1, w2], axis=0)
    return ids, wts


def _ln_mod_kernel(*refs, n_y, do_ln, out_h, do_route, alpha):
    refs = list(refs)
    x_ref = refs.pop(0)
    y_refs = [refs.pop(0) for _ in range(n_y)]
    if do_ln:
        gate_ref, lng_ref, lnb_ref = refs.pop(0), refs.pop(0), refs.pop(0)
    if out_h is not None:
        sc_ref, sh_ref = refs.pop(0), refs.pop(0)
    if do_route:
        whi_ref, wlo_ref, rb_ref = refs.pop(0), refs.pop(0), refs.pop(0)
    xo_ref = refs.pop(0) if do_ln else None
    h_ref = refs.pop(0) if out_h is not None else None
    if do_route:
        ids_ref, wts_ref = refs.pop(0), refs.pop(0)
    assert not refs

    x = x_ref[...]
    if do_ln:
        y = functools.reduce(jnp.add, [r[...] for r in y_refs])
        z = alpha * x + gate_ref[0] * y
        mu = jnp.mean(z, axis=-1, keepdims=True)
        zc = z - mu
        var = jnp.mean(zc * zc, axis=-1, keepdims=True)
        x = zc * lax.rsqrt(var + LN_EPS) * lng_ref[...] + lnb_ref[...]
        xo_ref[...] = x
    if out_h is not None:
        h = x * (1.0 + sc_ref[0]) + sh_ref[0]
        h_ref[...] = h.astype(out_h)
    if do_route:
        h_hi = h.astype(BF16)
        h_lo = (h - h_hi.astype(F32)).astype(BF16)
        w_hi = whi_ref[...]
        lg = (jnp.dot(h_hi, w_hi, preferred_element_type=F32)
              + jnp.dot(h_lo, w_hi, preferred_element_type=F32)
              + jnp.dot(h_hi, wlo_ref[...], preferred_element_type=F32)) + rb_ref[...]
        ids, wts = _route_rows(lg.T)
        pad = SUBLANES - ids.shape[0]
        ids_ref[...] = jnp.concatenate([ids, jnp.zeros((pad, ids.shape[1]), jnp.int32)], axis=0)
        wts_ref[...] = jnp.concatenate([wts, jnp.zeros((pad, wts.shape[1]), F32)], axis=0)


def _ln_mod(x, ys, gate, ln_g, ln_b, sc, sh, route_w, *, lc, alpha, out_h, row_offset=0):
    t_all, d = x.shape
    do_ln = len(ys) > 0
    do_route = route_w is not None
    tm = _tile(math.gcd(lc, t_all - lc), 128)
    assert row_offset % tm == 0
    off = row_offset // tm
    t_out = t_all - row_offset
    nct = lc // tm

    row = lambda i: (i + off, 0)
    stream = lambda i: (jnp.where(i + off < nct, 0, 1), 0, 0)
    vec = pl.BlockSpec((1, d), lambda i: (0, 0))
    svec = pl.BlockSpec((1, 1, d), stream)

    args, in_specs = [x], [pl.BlockSpec((tm, d), row)]
    n_y = 0
    for y in ys:
        for slab in range(y.shape[0] // t_all):
            n_y += 1
            args.append(y)
            in_specs.append(pl.BlockSpec((tm, d), lambda i, slab=slab: (i + off + slab * (t_all // tm), 0)))
    if do_ln:
        args += [gate, ln_g.reshape(1, d), ln_b.reshape(1, d)]
        in_specs += [svec, vec, vec]
    if out_h is not None:
        args += [sc, sh]
        in_specs += [svec, svec]
    if do_route:
        args += list(route_w)
        in_specs += [pl.BlockSpec((d, LANES), lambda i: (0, 0)),
                     pl.BlockSpec((d, LANES), lambda i: (0, 0)),
                     pl.BlockSpec((1, LANES), lambda i: (0, 0))]
    out_shape, out_specs = [], []
    if do_ln:
        out_shape.append(jax.ShapeDtypeStruct((t_out, d), F32))
        out_specs.append(pl.BlockSpec((tm, d), lambda i: (i, 0)))
    if out_h is not None:
        out_shape.append(jax.ShapeDtypeStruct((t_out, d), out_h))
        out_specs.append(pl.BlockSpec((tm, d), lambda i: (i, 0)))
    if do_route:
        out_shape += [jax.ShapeDtypeStruct((SUBLANES, t_out), jnp.int32),
                      jax.ShapeDtypeStruct((SUBLANES, t_out), F32)]
        out_specs += [pl.BlockSpec((SUBLANES, tm), lambda i: (0, i))] * 2
    return pl.pallas_call(
        functools.partial(_ln_mod_kernel, n_y=n_y, do_ln=do_ln, out_h=out_h,
                          do_route=do_route, alpha=alpha),
        grid=(t_out // tm,),
        in_specs=in_specs,
        out_specs=out_specs,
        out_shape=out_shape,
        compiler_params=_params("parallel"),
        name="ln_mod",
    )(*args)


def _proj_kernel(*refs, norm, rope, scale, tn):
    refs = list(refs)
    a_ref, w_ref = refs.pop(0), refs.pop(0)
    g_ref = refs.pop(0) if norm else None
    if rope:
        cos_ref, sin_ref = refs.pop(0), refs.pop(0)
    o_ref = refs.pop(0)
    acc = jnp.dot(a_ref[...], w_ref[...], preferred_element_type=F32)
    if not (norm or rope):
        if scale != 1.0:
            acc = acc * scale
        o_ref[...] = acc.astype(o_ref.dtype)
        return
    if rope:
        cos, sin = cos_ref[...], sin_ref[...]
        lane = lax.broadcasted_iota(jnp.int32, cos.shape, 1)
        first_half = (lane % (HEAD_DIM // 2)) < (HEAD_DIM // 4)
    for hh in range(tn // HEAD_DIM):
        cols = slice(hh * HEAD_DIM, (hh + 1) * HEAD_DIM)
        t = acc[:, cols]
        if norm:
            t = t * lax.rsqrt(jnp.mean(t * t, axis=-1, keepdims=True) + RMS_EPS) * g_ref[...]
        if rope:
            up = pltpu.roll(t, HEAD_DIM - HEAD_DIM // 4, 1)
            dn = pltpu.roll(t, HEAD_DIM // 4, 1)
            t = t * cos + jnp.where(first_half, up, dn) * sin
        if scale != 1.0:
            t = t * scale
        o_ref[:, cols] = t.astype(o_ref.dtype)


def _proj(a, w, *, out_dtype, norm_g=None, rope=None, scale=1.0, tm_target=768, tn_target=1024):
    t_all, kdim = a.shape
    n = w.shape[1]
    tm = _tile(t_all, tm_target)
    tn = _tile(n, tn_target, LANES)
    args = [a, w]
    in_specs = [pl.BlockSpec((tm, kdim), lambda i, j: (i, 0)),
                pl.BlockSpec((kdim, tn), lambda i, j: (0, j))]
    if norm_g is not None:
        args.append(norm_g.reshape(1, HEAD_DIM))
        in_specs.append(pl.BlockSpec((1, HEAD_DIM), lambda i, j: (0, 0)))
    if rope is not None:
        args += list(rope)
        in_specs += [pl.BlockSpec((tm, HEAD_DIM), lambda i, j: (i, 0))] * 2
    return pl.pallas_call(
        functools.partial(_proj_kernel, norm=norm_g is not None, rope=rope is not None,
                          scale=scale, tn=tn),
        grid=(t_all // tm, n // tn),
        in_specs=in_specs,
        out_specs=pl.BlockSpec((tm, tn), lambda i, j: (i, j)),
        out_shape=jax.ShapeDtypeStruct((t_all, n), out_dtype),
        compiler_params=_params("parallel", "parallel"),
        name="proj",
    )(*args)


def _rope_tables(n, lc):
    n_freq = HEAD_DIM // 4
    pos = jnp.arange(n, dtype=jnp.int32)
    row = (pos // GRID_W).astype(F32)
    col = (pos % GRID_W).astype(F32)
    inv_freq = ROPE_THETA ** (-jnp.arange(n_freq, dtype=F32) / n_freq)
    ar, ac = row[:, None] * inv_freq, col[:, None] * inv_freq
    cos = jnp.concatenate([jnp.cos(ar), jnp.cos(ar), jnp.cos(ac), jnp.cos(ac)], axis=1)
    sin = jnp.concatenate([-jnp.sin(ar), jnp.sin(ar), -jnp.sin(ac), jnp.sin(ac)], axis=1)
    cos = jnp.concatenate([jnp.ones((lc, HEAD_DIM), F32), cos], axis=0)
    sin = jnp.concatenate([jnp.zeros((lc, HEAD_DIM), F32), sin], axis=0)
    return cos, sin


def _sweep(qs, kcols, k_ref, v_ref, latent_rows, finish, *, lc, tk, n_lat, dv, sum_on_mxu):
    nq = len(qs)
    n = n_lat // tk

    def e0(size):
        return (lax.broadcasted_iota(jnp.int32, (size, LANES), 1) == 0).astype(v_ref.dtype)

    ones_c, ones_t = (e0(lc), e0(tk)) if sum_on_mxu else (None, None)

    def scores(lo, size):
        return [lax.dot_general(q, k_ref[pl.ds(lo, size), kc], (((1,), (1,)), ((), ())),
                                preferred_element_type=F32) for q, kc in zip(qs, kcols)]

    def pv(p, lo, size, ones):
        vc = v_ref[pl.ds(lo, size), :]
        if sum_on_mxu:
            vc = jnp.concatenate([vc, ones], axis=1)
        return jnp.dot(p, vc, preferred_element_type=F32)

    def soft(s, m, l):
        m_new = jnp.maximum(m, jnp.max(s, axis=-1, keepdims=True))
        a = jnp.exp2(m - m_new)
        p = jnp.exp2(s - m_new)
        if not sum_on_mxu:
            l = a * l + jnp.sum(p, axis=-1, keepdims=True)
        return m_new, a, p.astype(v_ref.dtype), l

    def result(acc, l):
        return acc[:, :dv] / acc[:, dv:dv + 1] if sum_on_mxu else acc / l

    ctx = []
    for s in scores(0, lc):
        m = jnp.max(s, axis=-1, keepdims=True)
        p = jnp.exp2(s - m)
        l = None if sum_on_mxu else jnp.sum(p, axis=-1, keepdims=True)
        ctx.append((m, l, pv(p.astype(v_ref.dtype), 0, lc, ones_c)))

    @pl.when(jnp.logical_not(latent_rows))
    def _():
        finish([result(acc, l) for _, l, acc in ctx])

    @pl.when(latent_rows)
    def _():
        def start(c, s):
            m0, l0, acc0 = ctx[c]
            m, a, p, l = soft(s, m0, l0)
            return (m, p, a * acc0) if sum_on_mxu else (m, l, p, a * acc0)

        state = tuple(start(c, s) for c, s in enumerate(scores(lc, tk)))

        def body(j, state):
            lo = pl.multiple_of(lc + j * tk, math.gcd(lc, tk))
            ss = scores(lo, tk)
            new = []
            for c in range(nq):
                m0, l0 = state[c][0], None if sum_on_mxu else state[c][1]
                acc = state[c][-1] + pv(state[c][-2], lo - tk, tk, ones_t)
                m, a, p, l = soft(ss[c], m0, l0)
                new.append((m, p, a * acc) if sum_on_mxu else (m, l, p, a * acc))
            return tuple(new)

        state = lax.fori_loop(1, n, body, state)
        outs = []
        for c in range(nq):
            l = None if sum_on_mxu else state[c][1]
            p, r = state[c][-2], state[c][-1]
            outs.append(result(r + pv(p, lc + (n - 1) * tk, tk, ones_t), l))
        finish(outs)


def _gqa_kernel(q_ref, k_ref, v_ref, o_ref, *, lc, tq, tk, n_lat):
    i = pl.program_id(1)
    heads = [slice(h * HEAD_DIM, (h + 1) * HEAD_DIM) for h in range(GQA_GROUP)]

    def finish(outs):
        for cols, o in zip(heads, outs):
            o_ref[:, cols] = o.astype(o_ref.dtype)

    _sweep([q_ref[:, cols] for cols in heads], [heads[0]] * GQA_GROUP, k_ref, v_ref, i * tq >= lc, finish,
           lc=lc, tk=tk, n_lat=n_lat, dv=HEAD_DIM, sum_on_mxu=True)


def _gqa_attention(q, k, v, *, lc, tq_target=256, tk_target=1024):
    t_all, dq = q.shape
    n_kv = k.shape[1] // HEAD_DIM
    n_lat = t_all - lc
    tq = _tile(math.gcd(lc, n_lat), tq_target)
    tk = _tile(n_lat, tk_target)
    gw = GQA_GROUP * HEAD_DIM
    return pl.pallas_call(
        functools.partial(_gqa_kernel, lc=lc, tq=tq, tk=tk, n_lat=n_lat),
        grid=(n_kv, t_all // tq),
        in_specs=[pl.BlockSpec((tq, gw), lambda g, i: (i, g)),
                  pl.BlockSpec((t_all, HEAD_DIM), lambda g, i: (0, g)),
                  pl.BlockSpec((t_all, HEAD_DIM), lambda g, i: (0, g))],
        out_specs=pl.BlockSpec((tq, gw), lambda g, i: (i, g)),
        out_shape=jax.ShapeDtypeStruct((t_all, dq), BF16),
        compiler_params=_params("parallel", "parallel"),
        name="gqa_attention",
    )(q, k, v)


def _diff_kernel(q_ref, k_ref, v_ref, lq1_ref, lk1_ref, lq2_ref, lk2_ref, g_ref, o_ref,
                 *, lc, tq, tk, n_lat, lam_init):
    i = pl.program_id(1)
    dv = 2 * HEAD_DIM

    def finish(outs):
        lam = (jnp.exp(jnp.sum(lq1_ref[...] * lk1_ref[...], axis=-1, keepdims=True))
               - jnp.exp(jnp.sum(lq2_ref[...] * lk2_ref[...], axis=-1, keepdims=True)) + lam_init)
        o = outs[0] - lam * outs[1]
        o = o * lax.rsqrt(jnp.mean(o * o, axis=-1, keepdims=True) + RMS_EPS) * g_ref[...]
        o_ref[...] = (o * (1.0 - lam_init)).astype(o_ref.dtype)

    halves = [slice(0, HEAD_DIM), slice(HEAD_DIM, dv)]
    _sweep([q_ref[:, cols] for cols in halves], halves, k_ref, v_ref, i * tq >= lc, finish,
           lc=lc, tk=tk, n_lat=n_lat, dv=dv, sum_on_mxu=False)


def _diff_attention(q, k, v, lq1, lk1, lq2, lk2, subln_g, *, lc, lam_init, tq_target=256, tk_target=1024):
    t_all, dq = q.shape
    dv = 2 * HEAD_DIM
    n_heads = dq // dv
    n_lat = t_all - lc
    tq = _tile(math.gcd(lc, n_lat), tq_target)
    tk = _tile(n_lat, tk_target)
    vec = lambda n: pl.BlockSpec((1, n), lambda h, i: (0, 0))
    return pl.pallas_call(
        functools.partial(_diff_kernel, lc=lc, tq=tq, tk=tk, n_lat=n_lat, lam_init=lam_init),
        grid=(n_heads, t_all // tq),
        in_specs=[pl.BlockSpec((tq, dv), lambda h, i: (i, h)),
                  pl.BlockSpec((t_all, dv), lambda h, i: (0, h)),
                  pl.BlockSpec((t_all, dv), lambda h, i: (0, h)),
                  vec(HEAD_DIM), vec(HEAD_DIM), vec(HEAD_DIM), vec(HEAD_DIM), vec(dv)],
        out_specs=pl.BlockSpec((tq, dv), lambda h, i: (i, h)),
        out_shape=jax.ShapeDtypeStruct((t_all, v.shape[1]), BF16),
        compiler_params=_params("parallel", "parallel"),
        name="diff_attention",
    )(q, k, v, lq1.reshape(1, -1), lk1.reshape(1, -1), lq2.reshape(1, -1), lk2.reshape(1, -1),
      subln_g.reshape(1, -1))


def _pool_kernel(x_ref, xp_ref, xn_ref, sc_ref, sh_ref, w_ref, b_ref, ps_ref, o_ref, buf_ref,
                 *, lc, n_lat, tm):
    i = pl.program_id(0)
    nct = lc // tm
    nt = (lc + n_lat) // tm
    sc, sh = sc_ref[0], sh_ref[0]
    first = jnp.logical_or(i == 0, i == nct)
    last = jnp.logical_or(i == nct - 1, i == nt - 1)
    cur = x_ref[...] * (1.0 + sc) + sh
    buf_ref[0:POOL_HALO, :] = jnp.where(first, 0.0, xp_ref[...] * (1.0 + sc) + sh)
    buf_ref[POOL_HALO:POOL_HALO + tm, :] = cur
    buf_ref[POOL_HALO + tm:2 * POOL_HALO + tm, :] = jnp.where(last, 0.0, xn_ref[...] * (1.0 + sc) + sh)

    in_ctx = i < nct
    t_loc = (lax.broadcasted_iota(jnp.int32, (tm, 1), 0)
             + (i - jnp.where(in_ctx, 0, nct)) * tm)
    n_s = jnp.where(in_ctx, lc, n_lat)
    cdim = w_ref.shape[1]
    for g, win in enumerate(POOL_WINDOWS):
        cols = slice(g * cdim, (g + 1) * cdim)
        half = win // 2
        acc = buf_ref[POOL_HALO - half:POOL_HALO - half + tm, cols]
        for k in range(-half + 1, half):
            acc = acc + buf_ref[POOL_HALO + k:POOL_HALO + k + tm, cols]
        cnt = jnp.minimum(t_loc + half - 1, n_s - 1) - jnp.maximum(t_loc - half, 0) + 1
        y = acc / cnt.astype(F32) - cur[:, cols]
        out = jnp.dot(y.astype(BF16), w_ref[g], preferred_element_type=F32) + b_ref[g]
        o_ref[:, cols] = out * ps_ref[:, cols]


def _pool_mixer(x, sc, sh, pool_w, pool_b, pool_scale, *, lc):
    t_all, d = x.shape
    n_lat = t_all - lc
    ng, cdim, _ = pool_w.shape
    tm = _tile(math.gcd(lc, n_lat), 128)
    hb = tm // POOL_HALO
    n_hb = t_all // POOL_HALO
    stream = lambda i: (jnp.where(i < lc // tm, 0, 1), 0, 0)
    return pl.pallas_call(
        functools.partial(_pool_kernel, lc=lc, n_lat=n_lat, tm=tm),
        grid=(t_all // tm,),
        in_specs=[pl.BlockSpec((tm, d), lambda i: (i, 0)),
                  pl.BlockSpec((POOL_HALO, d), lambda i: (jnp.maximum(i * hb - 1, 0), 0)),
                  pl.BlockSpec((POOL_HALO, d), lambda i: (jnp.minimum((i + 1) * hb, n_hb - 1), 0)),
                  pl.BlockSpec((1, 1, d), stream),
                  pl.BlockSpec((1, 1, d), stream),
                  pl.BlockSpec((ng, cdim, cdim), lambda i: (0, 0, 0)),
                  pl.BlockSpec((ng, 1, cdim), lambda i: (0, 0, 0)),
                  pl.BlockSpec((1, d), lambda i: (0, 0))],
        out_specs=pl.BlockSpec((tm, d), lambda i: (i, 0)),
        out_shape=jax.ShapeDtypeStruct((t_all, d), F32),
        scratch_shapes=[pltpu.VMEM((tm + 2 * POOL_HALO, d), F32)],
        compiler_params=_params("parallel"),
        name="pool_mixer",
    )(x, x, x, sc, sh, pool_w.astype(BF16), pool_b.reshape(ng, 1, cdim), pool_scale.reshape(1, d))


def _moe_kernel(texp_ref, rtok_ref, rdst_ref, nval_ref, nact_ref,
                h_hbm, gate_ref, w1_ref, w3_ref, w2_ref, y_hbm,
                xbuf, ybuf, sem_in, sem_out, *, tm):
    i = pl.program_id(0)
    n = nact_ref[0]
    slot = lax.rem(i, 2)

    def gather(tile, s, k, tok):
        return pltpu.make_async_copy(h_hbm.at[pl.ds(tok, 1)], xbuf.at[s, pl.ds(k, 1)], sem_in.at[s])

    def scatter(tile, s, k, dst):
        return pltpu.make_async_copy(ybuf.at[s, pl.ds(k, 1)], y_hbm.at[pl.ds(dst, 1)], sem_out.at[s])

    def each_row(count, fn):
        lax.fori_loop(0, count, lambda k, c: (fn(k), c)[1], 0)

    def start_gather(tile, s):
        each_row(tm, lambda k: gather(tile, s, k, rtok_ref[tile * tm + k]).start())

    def start_scatter(tile, s):
        each_row(nval_ref[tile], lambda k: scatter(tile, s, k, rdst_ref[tile * tm + k]).start())

    def wait_scatter(tile, s):
        each_row(nval_ref[tile], lambda k: scatter(tile, s, k, 0).wait())

    @pl.when(i == 0)
    def _():
        start_gather(0, 0)

    @pl.when(i < n)
    def _():
        each_row(tm, lambda k: gather(i, slot, k, 0).wait())

        @pl.when(i + 1 < n)
        def _():
            start_gather(i + 1, 1 - slot)

        @pl.when(i >= 2)
        def _():
            wait_scatter(i - 2, slot)

        x = xbuf[slot].astype(BF16)
        a = jnp.dot(x, w1_ref[0], preferred_element_type=F32)
        u = jnp.dot(x, w3_ref[0], preferred_element_type=F32)
        hid = a * _sigmoid(a) * u * gate_ref[...]
        ybuf[slot] = jnp.dot(hid.astype(BF16), w2_ref[0], preferred_element_type=F32)
        start_scatter(i, slot)

        @pl.when(i == n - 1)
        def _():
            wait_scatter(i, slot)

            @pl.when(i >= 1)
            def _():
                wait_scatter(i - 1, 1 - slot)


def _moe(h, ids, wts, w1, w3, w2, *, tm_target=256):
    t_all, d = h.shape
    n_exp, _, f = w1.shape
    tm = _tile(t_all, tm_target)
    n_pairs = 2 * t_all
    n_tiles = n_pairs // tm + n_exp
    n_rows = n_tiles * tm

    e_flat = ids.reshape(n_pairs)
    w_flat = wts.reshape(n_pairs)
    onehot = (e_flat[:, None] == jnp.arange(n_exp, dtype=jnp.int32)[None, :]).astype(jnp.int32)
    rank = jnp.sum((jnp.cumsum(onehot, axis=0) - onehot) * onehot, axis=1)
    counts = jnp.sum(onehot, axis=0)
    padded = (counts + tm - 1) // tm * tm
    p_end = jnp.cumsum(padded)
    pos = jnp.sum(onehot * (p_end - padded)[None, :], axis=1) + rank
    row_pair = jnp.full((n_rows,), -1, jnp.int32).at[pos].set(jnp.arange(n_pairs, dtype=jnp.int32))
    valid = row_pair >= 0
    safe = jnp.maximum(row_pair, 0)
    row_tok = jnp.where(valid, safe % t_all, 0)
    row_gate = jnp.where(valid, w_flat[safe], 0.0).reshape(n_rows, 1)
    tile_nval = jnp.sum(valid.reshape(n_tiles, tm), axis=1, dtype=jnp.int32)
    n_act = (p_end[-1] // tm).astype(jnp.int32)
    tile_start = jnp.arange(n_tiles, dtype=jnp.int32) * tm
    tile_exp = jnp.sum(p_end[None, :] <= tile_start[:, None], axis=1, dtype=jnp.int32)
    last_exp = jnp.sum(p_end <= (n_act - 1) * tm, dtype=jnp.int32)
    tile_exp = jnp.minimum(jnp.where(tile_start < n_act * tm, tile_exp, last_exp), n_exp - 1)

    wspec = lambda shape: pl.BlockSpec(shape, lambda i, te, rt, rd, nv, na: (te[i], 0, 0))
    return pl.pallas_call(
        functools.partial(_moe_kernel, tm=tm),
        grid_spec=pltpu.PrefetchScalarGridSpec(
            num_scalar_prefetch=5,
            grid=(n_tiles,),
            in_specs=[pl.BlockSpec(memory_space=pl.ANY),
                      pl.BlockSpec((tm, 1), lambda i, te, rt, rd, nv, na: (i, 0)),
                      wspec((1, d, f)), wspec((1, d, f)), wspec((1, f, d))],
            out_specs=pl.BlockSpec(memory_space=pl.ANY),
            scratch_shapes=[pltpu.VMEM((2, tm, d), F32), pltpu.VMEM((2, tm, d), F32),
                            pltpu.SemaphoreType.DMA((2,)), pltpu.SemaphoreType.DMA((2,))],
        ),
        out_shape=jax.ShapeDtypeStruct((n_pairs, d), F32),
        compiler_params=_params("arbitrary"),
        name="moe_experts",
    )(tile_exp, row_tok, row_pair, tile_nval, n_act.reshape(1), h, row_gate,
      w1.astype(BF16), w3.astype(BF16), w2.astype(BF16))


def _router_weights(rg_w, rg_b, re_w, re_b):
    d = rg_w.shape[0]
    n = rg_w.shape[1] + re_w.shape[1]
    w = jnp.concatenate([rg_w, re_w, jnp.zeros((d, LANES - n), F32)], axis=1)
    b = jnp.concatenate([rg_b, re_b, jnp.zeros((LANES - n,), F32)]).reshape(1, LANES)
    w_hi = w.astype(BF16)
    w_lo = (w - w_hi.astype(F32)).astype(BF16)
    return w_hi, w_lo, b


def kernel(x, c, ctx, c_ctx, ada_w, ada_b, ln_g, ln_b, attn_wq, attn_wk, attn_wv, attn_wo, attn_qn_g, attn_kn_g, pool_w, pool_b, pool_scale, diff_wq, diff_wk, diff_wv, diff_wo, diff_lq1, diff_lk1, diff_lq2, diff_lk2, diff_subln_g, moe_rg_w, moe_rg_b, moe_re_w, moe_re_b, moe_w1, moe_w3, moe_w2):
    b, n, d = x.shape
    assert b == 1 and c.shape[0] == 1 and ctx.shape[0] == 1
    lc = ctx.shape[1]
    depth = ada_w.shape[0]
    alpha = (2 * depth) ** 0.25
    qk_scale = math.log2(math.e) / math.sqrt(HEAD_DIM)

    s = jnp.concatenate([ctx[0], x[0]], axis=0)
    mods = _ada_all(jnp.concatenate([c_ctx[None, :], c], axis=0), ada_w, ada_b)
    mod = lambda i, k: mods[i, :, k * d:(k + 1) * d].reshape(2, 1, d)
    rope = _rope_tables(n, lc)
    bf = lambda w: w.astype(BF16)

    h1 = None
    for i in range(depth):
        last = i == depth - 1
        kind, j = i % N_MIXERS, i // N_MIXERS
        if kind != 1 and h1 is None:
            (h1,) = _ln_mod(s, (), None, None, None, mod(i, 1), mod(i, 0), None,
                            lc=lc, alpha=alpha, out_h=BF16)
        if kind == 0:
            q = _proj(h1, bf(attn_wq[j]), out_dtype=BF16, norm_g=attn_qn_g[j], rope=rope, scale=qk_scale)
            k = _proj(h1, bf(attn_wk[j]), out_dtype=BF16, norm_g=attn_kn_g[j], rope=rope)
            v = _proj(h1, bf(attn_wv[j]), out_dtype=BF16)
            o = _gqa_attention(q, k, v, lc=lc)
            y = _proj(o, bf(attn_wo[j]), out_dtype=F32)
        elif kind == 1:
            y = _pool_mixer(s, mod(i, 1), mod(i, 0), pool_w[j], pool_b[j], pool_scale[j], lc=lc)
        else:
            lam_init = 0.8 - 0.6 * math.exp(-0.3 * i)
            q = _proj(h1, bf(diff_wq[j]), out_dtype=BF16, rope=rope, scale=qk_scale)
            k = _proj(h1, bf(diff_wk[j]), out_dtype=BF16, rope=rope)
            v = _proj(h1, bf(diff_wv[j]), out_dtype=BF16)
            o = _diff_attention(q, k, v, diff_lq1[j], diff_lk1[j], diff_lq2[j], diff_lk2[j],
                                diff_subln_g[j], lc=lc, lam_init=lam_init)
            y = _proj(o, bf(diff_wo[j]), out_dtype=F32)
        h1 = None
        route_w = _router_weights(moe_rg_w[i], moe_rg_b[i], moe_re_w[i], moe_re_b[i])
        s, h2, ids, wts = _ln_mod(s, (y,), mod(i, 2), ln_g[i, 0], ln_b[i, 0], mod(i, 4), mod(i, 3),
                                  route_w, lc=lc, alpha=alpha, out_h=F32)
        y2 = _moe(h2, ids[:2], wts[:2], moe_w1[i], moe_w3[i], moe_w2[i])
        if last:
            (s,) = _ln_mod(s, (y2,), mod(i, 5), ln_g[i, 1], ln_b[i, 1], None, None, None,
                           lc=lc, alpha=alpha, out_h=None, row_offset=lc)
        elif (i + 1) % N_MIXERS == 1:
            (s,) = _ln_mod(s, (y2,), mod(i, 5), ln_g[i, 1], ln_b[i, 1], None, None, None,
                           lc=lc, alpha=alpha, out_h=None)
        else:
            s, h1 = _ln_mod(s, (y2,), mod(i, 5), ln_g[i, 1], ln_b[i, 1], mod(i + 1, 1), mod(i + 1, 0),
                            None, lc=lc, alpha=alpha, out_h=BF16)
    return s[None]
```

```python
import functools
import math

import jax
import jax.numpy as jnp
from jax import lax
from jax.experimental import pallas as pl
from jax.experimental.pallas import tpu as pltpu

HEAD_DIM = 128
GRID_W = 64
ROPE_THETA = 10000.0
POOL_WINDOWS = (2, 4, 8, 16)
POOL_HALO = 8
N_EXPERT_GROUPS = 4
EXPERTS_PER_GROUP = 4
N_EXPERTS = N_EXPERT_GROUPS * EXPERTS_PER_GROUP
N_MIXERS = 3
GQA_GROUP = 4
LN_EPS = 1e-6
RMS_EPS = 1e-6
LANES = 128
SUBLANES = 8
VMEM_LIMIT_BYTES = 56 * 1024 * 1024

F32 = jnp.float32
BF16 = jnp.bfloat16


def _params(*sem):
    return pltpu.CompilerParams(dimension_semantics=sem, vmem_limit_bytes=VMEM_LIMIT_BYTES)


def _tile(n, target, mult=SUBLANES):
    best = None
    for t in range(mult, min(n, target) + 1, mult):
        if n % t == 0:
            best = t
    assert best is not None, (n, target, mult)
    return best


def _sigmoid(v):
    return 1.0 / (1.0 + jnp.exp(-v))


def _ada_kernel(c_ref, w_ref, b_ref, o_ref, acc_ref, *, tk, tn):
    k = pl.program_id(2)

    @pl.when(k == 0)
    def _():
        acc_ref[...] = jnp.zeros_like(acc_ref)

    cv = c_ref[...]
    sv = cv * _sigmoid(cv)
    s0, s1 = sv[0], sv[1]
    for j in range(tn // LANES):
        cols = slice(j * LANES, (j + 1) * LANES)
        wj = w_ref[0, :, cols]
        acc_ref[0, :, cols] += (wj * s0).reshape(tk // SUBLANES, SUBLANES, LANES).sum(axis=0)
        acc_ref[1, :, cols] += (wj * s1).reshape(tk // SUBLANES, SUBLANES, LANES).sum(axis=0)

    @pl.when(k == pl.num_programs(2) - 1)
    def _():
        o_ref[0] = acc_ref[...].sum(axis=1) + b_ref[0]


def _ada_all(cvec, ada_w, ada_b):
    n_layers, d, n_out = ada_w.shape
    tk = _tile(d, 1024)
    tn = _tile(n_out, 2048, LANES)
    c_rep = jnp.broadcast_to(cvec[:, :, None], (2, d, LANES))
    return pl.pallas_call(
        functools.partial(_ada_kernel, tk=tk, tn=tn),
        grid=(n_layers, n_out // tn, d // tk),
        in_specs=[
            pl.BlockSpec((2, tk, LANES), lambda l, j, k: (0, k, 0)),
            pl.BlockSpec((1, tk, tn), lambda l, j, k: (l, k, j)),
            pl.BlockSpec((1, 1, tn), lambda l, j, k: (l, 0, j)),
        ],
        out_specs=pl.BlockSpec((1, 2, tn), lambda l, j, k: (l, 0, j)),
        out_shape=jax.ShapeDtypeStruct((n_layers, 2, n_out), F32),
        scratch_shapes=[pltpu.VMEM((2, SUBLANES, tn), F32)],
        compiler_params=_params("parallel", "parallel", "arbitrary"),
        name="ada_mod",
    )(c_rep, ada_w, ada_b.reshape(n_layers, 1, n_out))


def _route_rows(lg):
    ng, ne = N_EXPERT_GROUPS, EXPERTS_PER_GROUP
    g = [lg[r:r + 1, :] for r in range(ng)]
    gmax = functools.reduce(jnp.maximum, g)
    gidx = jnp.full(gmax.shape, ng - 1, jnp.int32)
    for r in range(ng - 2, -1, -1):
        gidx = jnp.where(g[r] == gmax, r, gidx)
    p_g = 1.0 / functools.reduce(jnp.add, [jnp.exp(gr - gmax) for gr in g])
    sel = []
    for j in range(ne):
        v = lg[ng + (ng - 1) * ne + j:ng + (ng - 1) * ne + j + 1, :]
        for r in range(ng - 2, -1, -1):
            v = jnp.where(gidx == r, lg[ng + r * ne + j:ng + r * ne + j + 1, :], v)
        sel.append(v)
    v1 = functools.reduce(jnp.maximum, sel)
    i1 = jnp.full(v1.shape, ne - 1, jnp.int32)
    for j in range(ne - 2, -1, -1):
        i1 = jnp.where(sel[j] == v1, j, i1)
    rest = [jnp.where(i1 == j, -jnp.inf, sel[j]) for j in range(ne)]
    v2 = functools.reduce(jnp.maximum, rest)
    i2 = jnp.full(v2.shape, ne - 1, jnp.int32)
    for j in range(ne - 2, -1, -1):
        i2 = jnp.where(rest[j] == v2, j, i2)
    t = jnp.exp(v2 - v1)
    w1 = p_g / (1.0 + t)
    w2 = p_g * t / (1.0 + t)
    ids = jnp.concatenate([gidx * ne + i1, gidx * ne + i2], axis=0)
    wts = jnp.concatenate([w1, w2], axis=0)
    return ids, wts


def _ln_mod_kernel(*refs, n_y, do_ln, out_h, do_route, alpha):
    refs = list(refs)
    x_ref = refs.pop(0)
    y_refs = [refs.pop(0) for _ in range(n_y)]
    if do_ln:
        gate_ref, lng_ref, lnb_ref = refs.pop(0), refs.pop(0), refs.pop(0)
    if out_h is not None:
        sc_ref, sh_ref = refs.pop(0), refs.pop(0)
    if do_route:
        whi_ref, wlo_ref, rb_ref = refs.pop(0), refs.pop(0), refs.pop(0)
    xo_ref = refs.pop(0) if do_ln else None
    h_ref = refs.pop(0) if out_h is not None else None
    if do_route:
        ids_ref, wts_ref = refs.pop(0), refs.pop(0)
    assert not refs

    x = x_ref[...]
    if do_ln:
        y = functools.reduce(jnp.add, [r[...] for r in y_refs])
        z = alpha * x + gate_ref[0] * y
        mu = jnp.mean(z, axis=-1, keepdims=True)
        zc = z - mu
        var = jnp.mean(zc * zc, axis=-1, keepdims=True)
        x = zc * lax.rsqrt(var + LN_EPS) * lng_ref[...] + lnb_ref[...]
        xo_ref[...] = x
    if out_h is not None:
        h = x * (1.0 + sc_ref[0]) + sh_ref[0]
        h_ref[...] = h.astype(out_h)
    if do_route:
        h_hi = h.astype(BF16)
        h_lo = (h - h_hi.astype(F32)).astype(BF16)
        w_hi = whi_ref[...]
        lg = (jnp.dot(h_hi, w_hi, preferred_element_type=F32)
              + jnp.dot(h_lo, w_hi, preferred_element_type=F32)
              + jnp.dot(h_hi, wlo_ref[...], preferred_element_type=F32)) + rb_ref[...]
        ids, wts = _route_rows(lg.T)
        pad = SUBLANES - ids.shape[0]
        ids_ref[...] = jnp.concatenate([ids, jnp.zeros((pad, ids.shape[1]), jnp.int32)], axis=0)
        wts_ref[...] = jnp.concatenate([wts, jnp.zeros((pad, wts.shape[1]), F32)], axis=0)


def _ln_mod(x, ys, gate, ln_g, ln_b, sc, sh, route_w, *, lc, alpha, out_h, row_offset=0):
    t_all, d = x.shape
    do_ln = len(ys) > 0
    do_route = route_w is not None
    tm = _tile(math.gcd(lc, t_all - lc), 128)
    assert row_offset % tm == 0
    off = row_offset // tm
    t_out = t_all - row_offset
    nct = lc // tm

    row = lambda i: (i + off, 0)
    stream = lambda i: (jnp.where(i + off < nct, 0, 1), 0, 0)
    vec = pl.BlockSpec((1, d), lambda i: (0, 0))
    svec = pl.BlockSpec((1, 1, d), stream)

    args, in_specs = [x], [pl.BlockSpec((tm, d), row)]
    n_y = 0
    for y in ys:
        for slab in range(y.shape[0] // t_all):
            n_y += 1
            args.append(y)
            in_specs.append(pl.BlockSpec((tm, d), lambda i, slab=slab: (i + off + slab * (t_all // tm), 0)))
    if do_ln:
        args += [gate, ln_g.reshape(1, d), ln_b.reshape(1, d)]
        in_specs += [svec, vec, vec]
    if out_h is not None:
        args += [sc, sh]
        in_specs += [svec, svec]
    if do_route:
        args += list(route_w)
        in_specs += [pl.BlockSpec((d, LANES), lambda i: (0, 0)),
                     pl.BlockSpec((d, LANES), lambda i: (0, 0)),
                     pl.BlockSpec((1, LANES), lambda i: (0, 0))]
    out_shape, out_specs = [], []
    if do_ln:
        out_shape.append(jax.ShapeDtypeStruct((t_out, d), F32))
        out_specs.append(pl.BlockSpec((tm, d), lambda i: (i, 0)))
    if out_h is not None:
        out_shape.append(jax.ShapeDtypeStruct((t_out, d), out_h))
        out_specs.append(pl.BlockSpec((tm, d), lambda i: (i, 0)))
    if do_route:
        out_shape += [jax.ShapeDtypeStruct((SUBLANES, t_out), jnp.int32),
                      jax.ShapeDtypeStruct((SUBLANES, t_out), F32)]
        out_specs += [pl.BlockSpec((SUBLANES, tm), lambda i: (0, i))] * 2
    return pl.pallas_call(
        functools.partial(_ln_mod_kernel, n_y=n_y, do_ln=do_ln, out_h=out_h,
                          do_route=do_route, alpha=alpha),
        grid=(t_out // tm,),
        in_specs=in_specs,
        out_specs=out_specs,
        out_shape=out_shape,
        compiler_params=_params("parallel"),
        name="ln_mod",
    )(*args)


def _proj_kernel(*refs, norm, rope, scale, tn):
    refs = list(refs)
    a_ref, w_ref = refs.pop(0), refs.pop(0)
    g_ref = refs.pop(0) if norm else None
    if rope:
        cos_ref, sin_ref = refs.pop(0), refs.pop(0)
    o_ref = refs.pop(0)
    acc = jnp.dot(a_ref[...], w_ref[...], preferred_element_type=F32)
    if not (norm or rope):
        if scale != 1.0:
            acc = acc * scale
        o_ref[...] = acc.astype(o_ref.dtype)
        return
    if rope:
        cos, sin = cos_ref[...], sin_ref[...]
        lane = lax.broadcasted_iota(jnp.int32, cos.shape, 1)
        first_half = (lane % (HEAD_DIM // 2)) < (HEAD_DIM // 4)
    for hh in range(tn // HEAD_DIM):
        cols = slice(hh * HEAD_DIM, (hh + 1) * HEAD_DIM)
        t = acc[:, cols]
        if norm:
            t = t * lax.rsqrt(jnp.mean(t * t, axis=-1, keepdims=True) + RMS_EPS) * g_ref[...]
        if rope:
            up = pltpu.roll(t, HEAD_DIM - HEAD_DIM // 4, 1)
            dn = pltpu.roll(t, HEAD_DIM // 4, 1)
            t = t * cos + jnp.where(first_half, up, dn) * sin
        if scale != 1.0:
            t = t * scale
        o_ref[:, cols] = t.astype(o_ref.dtype)


def _proj(a, w, *, out_dtype, norm_g=None, rope=None, scale=1.0, tm_target=768, tn_target=1024):
    t_all, kdim = a.shape
    n = w.shape[1]
    tm = _tile(t_all, tm_target)
    tn = _tile(n, tn_target, LANES)
    args = [a, w]
    in_specs = [pl.BlockSpec((tm, kdim), lambda i, j: (i, 0)),
                pl.BlockSpec((kdim, tn), lambda i, j: (0, j))]
    if norm_g is not None:
        args.append(norm_g.reshape(1, HEAD_DIM))
        in_specs.append(pl.BlockSpec((1, HEAD_DIM), lambda i, j: (0, 0)))
    if rope is not None:
        args += list(rope)
        in_specs += [pl.BlockSpec((tm, HEAD_DIM), lambda i, j: (i, 0))] * 2
    return pl.pallas_call(
        functools.partial(_proj_kernel, norm=norm_g is not None, rope=rope is not None,
                          scale=scale, tn=tn),
        grid=(t_all // tm, n // tn),
        in_specs=in_specs,
        out_specs=pl.BlockSpec((tm, tn), lambda i, j: (i, j)),
        out_shape=jax.ShapeDtypeStruct((t_all, n), out_dtype),
        compiler_params=_params("parallel", "parallel"),
        name="proj",
    )(*args)


def _rope_tables(n, lc):
    n_freq = HEAD_DIM // 4
    pos = jnp.arange(n, dtype=jnp.int32)
    row = (pos // GRID_W).astype(F32)
    col = (pos % GRID_W).astype(F32)
    inv_freq = ROPE_THETA ** (-jnp.arange(n_freq, dtype=F32) / n_freq)
    ar, ac = row[:, None] * inv_freq, col[:, None] * inv_freq
    cos = jnp.concatenate([jnp.cos(ar), jnp.cos(ar), jnp.cos(ac), jnp.cos(ac)], axis=1)
    sin = jnp.concatenate([-jnp.sin(ar), jnp.sin(ar), -jnp.sin(ac), jnp.sin(ac)], axis=1)
    cos = jnp.concatenate([jnp.ones((lc, HEAD_DIM), F32), cos], axis=0)
    sin = jnp.concatenate([jnp.zeros((lc, HEAD_DIM), F32), sin], axis=0)
    return cos, sin


def _sweep(qs, kcols, k_ref, v_ref, latent_rows, finish, *, lc, tk, n_lat, dv, sum_on_mxu):
    nq = len(qs)
    n = n_lat // tk

    def e0(size):
        return (lax.broadcasted_iota(jnp.int32, (size, LANES), 1) == 0).astype(v_ref.dtype)

    ones_c, ones_t = (e0(lc), e0(tk)) if sum_on_mxu else (None, None)

    def scores(lo, size):
        return [lax.dot_general(q, k_ref[pl.ds(lo, size), kc], (((1,), (1,)), ((), ())),
                                preferred_element_type=F32) for q, kc in zip(qs, kcols)]

    def pv(p, lo, size, ones):
        vc = v_ref[pl.ds(lo, size), :]
        if sum_on_mxu:
            vc = jnp.concatenate([vc, ones], axis=1)
        return jnp.dot(p, vc, preferred_element_type=F32)

    def soft(s, m, l):
        m_new = jnp.maximum(m, jnp.max(s, axis=-1, keepdims=True))
        a = jnp.exp2(m - m_new)
        p = jnp.exp2(s - m_new)
        if not sum_on_mxu:
            l = a * l + jnp.sum(p, axis=-1, keepdims=True)
        return m_new, a, p.astype(v_ref.dtype), l

    def result(acc, l):
        return acc[:, :dv] / acc[:, dv:dv + 1] if sum_on_mxu else acc / l

    ctx = []
    for s in scores(0, lc):
        m = jnp.max(s, axis=-1, keepdims=True)
        p = jnp.exp2(s - m)
        l = None if sum_on_mxu else jnp.sum(p, axis=-1, keepdims=True)
        ctx.append((m, l, pv(p.astype(v_ref.dtype), 0, lc, ones_c)))

    @pl.when(jnp.logical_not(latent_rows))
    def _():
        finish([result(acc, l) for _, l, acc in ctx])

    @pl.when(latent_rows)
    def _():
        def start(c, s):
            m0, l0, acc0 = ctx[c]
            m, a, p, l = soft(s, m0, l0)
            return (m, p, a * acc0) if sum_on_mxu else (m, l, p, a * acc0)

        state = tuple(start(c, s) for c, s in enumerate(scores(lc, tk)))

        def body(j, state):
            lo = pl.multiple_of(lc + j * tk, math.gcd(lc, tk))
            ss = scores(lo, tk)
            new = []
            for c in range(nq):
                m0, l0 = state[c][0], None if sum_on_mxu else state[c][1]
                acc = state[c][-1] + pv(state[c][-2], lo - tk, tk, ones_t)
                m, a, p, l = soft(ss[c], m0, l0)
                new.append((m, p, a * acc) if sum_on_mxu else (m, l, p, a * acc))
            return tuple(new)

        state = lax.fori_loop(1, n, body, state, unroll=True)
        outs = []
        for c in range(nq):
            l = None if sum_on_mxu else state[c][1]
            p, r = state[c][-2], state[c][-1]
            outs.append(result(r + pv(p, lc + (n - 1) * tk, tk, ones_t), l))
        finish(outs)


def _gqa_kernel(q_ref, k_ref, v_ref, o_ref, *, lc, tq, tk, n_lat):
    i = pl.program_id(1)
    heads = [slice(h * HEAD_DIM, (h + 1) * HEAD_DIM) for h in range(GQA_GROUP)]

    def finish(outs):
        for cols, o in zip(heads, outs):
            o_ref[:, cols] = o.astype(o_ref.dtype)

    _sweep([q_ref[:, cols] for cols in heads], [heads[0]] * GQA_GROUP, k_ref, v_ref, i * tq >= lc, finish,
           lc=lc, tk=tk, n_lat=n_lat, dv=HEAD_DIM, sum_on_mxu=True)


def _gqa_attention(q, k, v, *, lc, tq_target=256, tk_target=2048):
    t_all, dq = q.shape
    n_kv = k.shape[1] // HEAD_DIM
    n_lat = t_all - lc
    tq = _tile(math.gcd(lc, n_lat), tq_target)
    tk = _tile(n_lat, tk_target)
    gw = GQA_GROUP * HEAD_DIM
    return pl.pallas_call(
        functools.partial(_gqa_kernel, lc=lc, tq=tq, tk=tk, n_lat=n_lat),
        grid=(n_kv, t_all // tq),
        in_specs=[pl.BlockSpec((tq, gw), lambda g, i: (i, g)),
                  pl.BlockSpec((t_all, HEAD_DIM), lambda g, i: (0, g)),
                  pl.BlockSpec((t_all, HEAD_DIM), lambda g, i: (0, g))],
        out_specs=pl.BlockSpec((tq, gw), lambda g, i: (i, g)),
        out_shape=jax.ShapeDtypeStruct((t_all, dq), BF16),
        compiler_params=_params("parallel", "parallel"),
        name="gqa_attention",
    )(q, k, v)


def _diff_kernel(q_ref, k_ref, v_ref, lq1_ref, lk1_ref, lq2_ref, lk2_ref, g_ref, o_ref,
                 *, lc, tq, tk, n_lat, lam_init):
    i = pl.program_id(1)
    dv = 2 * HEAD_DIM

    def finish(outs):
        lam = (jnp.exp(jnp.sum(lq1_ref[...] * lk1_ref[...], axis=-1, keepdims=True))
               - jnp.exp(jnp.sum(lq2_ref[...] * lk2_ref[...], axis=-1, keepdims=True)) + lam_init)
        o = outs[0] - lam * outs[1]
        o = o * lax.rsqrt(jnp.mean(o * o, axis=-1, keepdims=True) + RMS_EPS) * g_ref[...]
        o_ref[...] = (o * (1.0 - lam_init)).astype(o_ref.dtype)

    halves = [slice(0, HEAD_DIM), slice(HEAD_DIM, dv)]
    _sweep([q_ref[:, cols] for cols in halves], halves, k_ref, v_ref, i * tq >= lc, finish,
           lc=lc, tk=tk, n_lat=n_lat, dv=dv, sum_on_mxu=False)


def _diff_attention(q, k, v, lq1, lk1, lq2, lk2, subln_g, *, lc, lam_init, tq_target=256, tk_target=1024):
    t_all, dq = q.shape
    dv = 2 * HEAD_DIM
    n_heads = dq // dv
    n_lat = t_all - lc
    tq = _tile(math.gcd(lc, n_lat), tq_target)
    tk = _tile(n_lat, tk_target)
    vec = lambda n: pl.BlockSpec((1, n), lambda h, i: (0, 0))
    return pl.pallas_call(
        functools.partial(_diff_kernel, lc=lc, tq=tq, tk=tk, n_lat=n_lat, lam_init=lam_init),
        grid=(n_heads, t_all // tq),
        in_specs=[pl.BlockSpec((tq, dv), lambda h, i: (i, h)),
                  pl.BlockSpec((t_all, dv), lambda h, i: (0, h)),
                  pl.BlockSpec((t_all, dv), lambda h, i: (0, h)),
                  vec(HEAD_DIM), vec(HEAD_DIM), vec(HEAD_DIM), vec(HEAD_DIM), vec(dv)],
        out_specs=pl.BlockSpec((tq, dv), lambda h, i: (i, h)),
        out_shape=jax.ShapeDtypeStruct((t_all, v.shape[1]), BF16),
        compiler_params=_params("parallel", "parallel"),
        name="diff_attention",
    )(q, k, v, lq1.reshape(1, -1), lk1.reshape(1, -1), lq2.reshape(1, -1), lk2.reshape(1, -1),
      subln_g.reshape(1, -1))


def _pool_kernel(x_ref, xp_ref, xn_ref, sc_ref, sh_ref, w_ref, b_ref, ps_ref, o_ref, buf_ref,
                 *, lc, n_lat, tm):
    i = pl.program_id(0)
    nct = lc // tm
    nt = (lc + n_lat) // tm
    sc, sh = sc_ref[0], sh_ref[0]
    first = jnp.logical_or(i == 0, i == nct)
    last = jnp.logical_or(i == nct - 1, i == nt - 1)
    cur = x_ref[...] * (1.0 + sc) + sh
    buf_ref[0:POOL_HALO, :] = jnp.where(first, 0.0, xp_ref[...] * (1.0 + sc) + sh)
    buf_ref[POOL_HALO:POOL_HALO + tm, :] = cur
    buf_ref[POOL_HALO + tm:2 * POOL_HALO + tm, :] = jnp.where(last, 0.0, xn_ref[...] * (1.0 + sc) + sh)

    in_ctx = i < nct
    t_loc = (lax.broadcasted_iota(jnp.int32, (tm, 1), 0)
             + (i - jnp.where(in_ctx, 0, nct)) * tm)
    n_s = jnp.where(in_ctx, lc, n_lat)
    cdim = w_ref.shape[1]
    for g, win in enumerate(POOL_WINDOWS):
        cols = slice(g * cdim, (g + 1) * cdim)
        half = win // 2
        acc = buf_ref[POOL_HALO - half:POOL_HALO - half + tm, cols]
        for k in range(-half + 1, half):
            acc = acc + buf_ref[POOL_HALO + k:POOL_HALO + k + tm, cols]
        cnt = jnp.minimum(t_loc + half - 1, n_s - 1) - jnp.maximum(t_loc - half, 0) + 1
        y = acc / cnt.astype(F32) - cur[:, cols]
        out = jnp.dot(y.astype(BF16), w_ref[g], preferred_element_type=F32) + b_ref[g]
        o_ref[:, cols] = out * ps_ref[:, cols]


def _pool_mixer(x, sc, sh, pool_w, pool_b, pool_scale, *, lc):
    t_all, d = x.shape
    n_lat = t_all - lc
    ng, cdim, _ = pool_w.shape
    tm = _tile(math.gcd(lc, n_lat), 128)
    hb = tm // POOL_HALO
    n_hb = t_all // POOL_HALO
    stream = lambda i: (jnp.where(i < lc // tm, 0, 1), 0, 0)
    return pl.pallas_call(
        functools.partial(_pool_kernel, lc=lc, n_lat=n_lat, tm=tm),
        grid=(t_all // tm,),
        in_specs=[pl.BlockSpec((tm, d), lambda i: (i, 0)),
                  pl.BlockSpec((POOL_HALO, d), lambda i: (jnp.maximum(i * hb - 1, 0), 0)),
                  pl.BlockSpec((POOL_HALO, d), lambda i: (jnp.minimum((i + 1) * hb, n_hb - 1), 0)),
                  pl.BlockSpec((1, 1, d), stream),
                  pl.BlockSpec((1, 1, d), stream),
                  pl.BlockSpec((ng, cdim, cdim), lambda i: (0, 0, 0)),
                  pl.BlockSpec((ng, 1, cdim), lambda i: (0, 0, 0)),
                  pl.BlockSpec((1, d), lambda i: (0, 0))],
        out_specs=pl.BlockSpec((tm, d), lambda i: (i, 0)),
        out_shape=jax.ShapeDtypeStruct((t_all, d), F32),
        scratch_shapes=[pltpu.VMEM((tm + 2 * POOL_HALO, d), F32)],
        compiler_params=_params("parallel"),
        name="pool_mixer",
    )(x, x, x, sc, sh, pool_w.astype(BF16), pool_b.reshape(ng, 1, cdim), pool_scale.reshape(1, d))


def _moe_kernel(texp_ref, rtok_ref, rdst_ref, nact_ref,
                h_hbm, gate_ref, w1_ref, w3_ref, w2_ref, y_hbm,
                xbuf, ybuf, sem_in, sem_out, *, tm, n_tiles):
    i = pl.program_id(0)
    n = nact_ref[0]
    slot = lax.rem(i, 2)
    spare = y_hbm.shape[0] - 2 * tm

    def gather(tile, s, k):
        tok = rtok_ref[tile * tm + k]
        return pltpu.make_async_copy(h_hbm.at[pl.ds(tok, 1)], xbuf.at[s, pl.ds(k, 1)], sem_in.at[s])

    def scatter(tile, s, k):
        dst = rdst_ref[(tile + 1) * tm + k]
        return pltpu.make_async_copy(ybuf.at[s, pl.ds(k, 1)], y_hbm.at[pl.ds(dst, 1)], sem_out.at[s])

    def fill(s):
        return pltpu.make_async_copy(ybuf.at[s], y_hbm.at[pl.ds(spare + s * tm, tm)], sem_out.at[s])

    def looped(fn):
        lax.fori_loop(0, tm, lambda k, c: (fn(k), c)[1], 0)

    @pl.when(i == 0)
    def _():
        ybuf[...] = jnp.zeros_like(ybuf)
        fill(0).start()
        fill(1).start()
        looped(lambda k: gather(0, 0, k).start())
        fill(0).wait()
        fill(1).wait()

    @pl.when(jnp.logical_and(i >= 1, i < n))
    def _():
        for k in range(tm):
            scatter(i - 2, slot, k).wait()

    @pl.when(i < n)
    def _():
        for k in range(tm):
            gather(i, slot, k).wait()
        nxt = jnp.minimum(i + 1, n_tiles - 1)
        x = xbuf[slot].astype(BF16)
        for k in range(tm):
            gather(nxt, 1 - slot, k).start()
        a = jnp.dot(x, w1_ref[0], preferred_element_type=F32)
        u = jnp.dot(x, w3_ref[0], preferred_element_type=F32)
        for k in range(tm):
            scatter(i - 1, 1 - slot, k).start()
        hid = a * _sigmoid(a) * u * gate_ref[...]
        ybuf[slot] = jnp.dot(hid.astype(BF16), w2_ref[0], preferred_element_type=F32)

        @pl.when(i == n - 1)
        def _():
            looped(lambda k: scatter(i, slot, k).start())
            looped(lambda k: gather(nxt, 1 - slot, k).wait())
            looped(lambda k: scatter(i - 1, 1 - slot, k).wait())
            looped(lambda k: scatter(i, slot, k).wait())


def _moe(h, ids, wts, w1, w3, w2, *, tm_target=256):
    t_all, d = h.shape
    n_exp, _, f = w1.shape
    tm = _tile(t_all, tm_target)
    n_pairs = 2 * t_all
    n_tiles = n_pairs // tm + n_exp
    n_rows = n_tiles * tm

    e_flat = ids.reshape(n_pairs)
    w_flat = wts.reshape(n_pairs)
    onehot = (e_flat[:, None] == jnp.arange(n_exp, dtype=jnp.int32)[None, :]).astype(jnp.int32)
    rank = jnp.sum((jnp.cumsum(onehot, axis=0) - onehot) * onehot, axis=1)
    counts = jnp.sum(onehot, axis=0)
    padded = (counts + tm - 1) // tm * tm
    p_end = jnp.cumsum(padded)
    pos = jnp.sum(onehot * (p_end - padded)[None, :], axis=1) + rank
    row_pair = jnp.full((n_rows,), -1, jnp.int32).at[pos].set(jnp.arange(n_pairs, dtype=jnp.int32))
    valid = row_pair >= 0
    safe = jnp.maximum(row_pair, 0)
    row_tok = jnp.where(valid, safe % t_all, 0)
    row_gate = jnp.where(valid, w_flat[safe], 0.0).reshape(n_rows, 1)
    rows = jnp.arange(-tm, n_rows, dtype=jnp.int32)
    spare_row = n_pairs + (rows // tm) % 2 * tm + rows % tm
    row_dst = jnp.where(rows >= 0, jnp.where(jnp.concatenate([jnp.zeros((tm,), bool), valid]),
                                             jnp.concatenate([jnp.zeros((tm,), jnp.int32), row_pair]),
                                             spare_row), spare_row)
    n_act = (p_end[-1] // tm).astype(jnp.int32)
    tile_start = jnp.arange(n_tiles, dtype=jnp.int32) * tm
    tile_exp = jnp.sum(p_end[None, :] <= tile_start[:, None], axis=1, dtype=jnp.int32)
    last_exp = jnp.sum(p_end <= (n_act - 1) * tm, dtype=jnp.int32)
    tile_exp = jnp.minimum(jnp.where(tile_start < n_act * tm, tile_exp, last_exp), n_exp - 1)

    wspec = lambda shape: pl.BlockSpec(shape, lambda i, te, rt, rd, na: (te[i], 0, 0))
    return pl.pallas_call(
        functools.partial(_moe_kernel, tm=tm, n_tiles=n_tiles),
        grid_spec=pltpu.PrefetchScalarGridSpec(
            num_scalar_prefetch=4,
            grid=(n_tiles,),
            in_specs=[pl.BlockSpec(memory_space=pl.ANY),
                      pl.BlockSpec((tm, 1), lambda i, te, rt, rd, na: (i, 0)),
                      wspec((1, d, f)), wspec((1, d, f)), wspec((1, f, d))],
            out_specs=pl.BlockSpec(memory_space=pl.ANY),
            scratch_shapes=[pltpu.VMEM((2, tm, d), F32), pltpu.VMEM((2, tm, d), F32),
                            pltpu.SemaphoreType.DMA((2,)), pltpu.SemaphoreType.DMA((2,))],
        ),
        out_shape=jax.ShapeDtypeStruct((n_pairs + 2 * tm, d), F32),
        compiler_params=_params("arbitrary"),
        name="moe_experts",
    )(tile_exp, row_tok, row_dst, n_act.reshape(1), h, row_gate,
      w1.astype(BF16), w3.astype(BF16), w2.astype(BF16))


def _router_weights(rg_w, rg_b, re_w, re_b):
    d = rg_w.shape[0]
    n = rg_w.shape[1] + re_w.shape[1]
    w = jnp.concatenate([rg_w, re_w, jnp.zeros((d, LANES - n), F32)], axis=1)
    b = jnp.concatenate([rg_b, re_b, jnp.zeros((LANES - n,), F32)]).reshape(1, LANES)
    w_hi = w.astype(BF16)
    w_lo = (w - w_hi.astype(F32)).astype(BF16)
    return w_hi, w_lo, b


def kernel(x, c, ctx, c_ctx, ada_w, ada_b, ln_g, ln_b, attn_wq, attn_wk, attn_wv, attn_wo, attn_qn_g, attn_kn_g, pool_w, pool_b, pool_scale, diff_wq, diff_wk, diff_wv, diff_wo, diff_lq1, diff_lk1, diff_lq2, diff_lk2, diff_subln_g, moe_rg_w, moe_rg_b, moe_re_w, moe_re_b, moe_w1, moe_w3, moe_w2):
    b, n, d = x.shape
    assert b == 1 and c.shape[0] == 1 and ctx.shape[0] == 1
    lc = ctx.shape[1]
    depth = ada_w.shape[0]
    alpha = (2 * depth) ** 0.25
    qk_scale = math.log2(math.e) / math.sqrt(HEAD_DIM)

    s = jnp.concatenate([ctx[0], x[0]], axis=0)
    mods = _ada_all(jnp.concatenate([c_ctx[None, :], c], axis=0), ada_w, ada_b)
    mod = lambda i, k: mods[i, :, k * d:(k + 1) * d].reshape(2, 1, d)
    rope = _rope_tables(n, lc)
    bf = lambda w: w.astype(BF16)

    h1 = None
    for i in range(depth):
        last = i == depth - 1
        kind, j = i % N_MIXERS, i // N_MIXERS
        if kind != 1 and h1 is None:
            (h1,) = _ln_mod(s, (), None, None, None, mod(i, 1), mod(i, 0), None,
                            lc=lc, alpha=alpha, out_h=BF16)
        if kind == 0:
            q = _proj(h1, bf(attn_wq[j]), out_dtype=BF16, norm_g=attn_qn_g[j], rope=rope, scale=qk_scale)
            k = _proj(h1, bf(attn_wk[j]), out_dtype=BF16, norm_g=attn_kn_g[j], rope=rope)
            v = _proj(h1, bf(attn_wv[j]), out_dtype=BF16)
            o = _gqa_attention(q, k, v, lc=lc)
            y = _proj(o, bf(attn_wo[j]), out_dtype=F32)
        elif kind == 1:
            y = _pool_mixer(s, mod(i, 1), mod(i, 0), pool_w[j], pool_b[j], pool_scale[j], lc=lc)
        else:
            lam_init = 0.8 - 0.6 * math.exp(-0.3 * i)
            q = _proj(h1, bf(diff_wq[j]), out_dtype=BF16, rope=rope, scale=qk_scale)
            k = _proj(h1, bf(diff_wk[j]), out_dtype=BF16, rope=rope)
            v = _proj(h1, bf(diff_wv[j]), out_dtype=BF16)
            o = _diff_attention(q, k, v, diff_lq1[j], diff_lk1[j], diff_lq2[j], diff_lk2[j],
                                diff_subln_g[j], lc=lc, lam_init=lam_init)
            y = _proj(o, bf(diff_wo[j]), out_dtype=F32)
        h1 = None
        route_w = _router_weights(moe_rg_w[i], moe_rg_b[i], moe_re_w[i], moe_re_b[i])
        s, h2, ids, wts = _ln_mod(s, (y,), mod(i, 2), ln_g[i, 0], ln_b[i, 0], mod(i, 4), mod(i, 3),
                                  route_w, lc=lc, alpha=alpha, out_h=F32)
        y2 = _moe(h2, ids[:2], wts[:2], moe_w1[i], moe_w3[i], moe_w2[i])
        if last:
            (s,) = _ln_mod(s, (y2,), mod(i, 5), ln_g[i, 1], ln_b[i, 1], None, None, None,
                           lc=lc, alpha=alpha, out_h=None, row_offset=lc)
        elif (i + 1) % N_MIXERS == 1:
            (s,) = _ln_mod(s, (y2,), mod(i, 5), ln_g[i, 1], ln_b[i, 1], None, None, None,
                           lc=lc, alpha=alpha, out_h=None)
        else:
            s, h1 = _ln_mod(s, (y2,), mod(i, 5), ln_g[i, 1], ln_b[i, 1], mod(i + 1, 1), mod(i + 1, 0),
                            None, lc=lc, alpha=alpha, out_h=BF16)
    return s[None]
```

```python
import functools
import math

import jax
import jax.numpy as jnp
from jax import lax
from jax.experimental import pallas as pl
from jax.experimental.pallas import tpu as pltpu

HEAD_DIM = 128
GRID_W = 64
ROPE_THETA = 10000.0
POOL_WINDOWS = (2, 4, 8, 16)
POOL_HALO = 8
N_EXPERT_GROUPS = 4
EXPERTS_PER_GROUP = 4
N_EXPERTS = N_EXPERT_GROUPS * EXPERTS_PER_GROUP
N_MIXERS = 3
GQA_GROUP = 4
LN_EPS = 1e-6
RMS_EPS = 1e-6
LANES = 128
SUBLANES = 8
VMEM_LIMIT_BYTES = 56 * 1024 * 1024

F32 = jnp.float32
BF16 = jnp.bfloat16


def _params(*sem):
    return pltpu.CompilerParams(dimension_semantics=sem, vmem_limit_bytes=VMEM_LIMIT_BYTES)


def _tile(n, target, mult=SUBLANES):
    best = None
    for t in range(mult, min(n, target) + 1, mult):
        if n % t == 0:
            best = t
    assert best is not None, (n, target, mult)
    return best


def _sigmoid(v):
    return 1.0 / (1.0 + jnp.exp(-v))


def _store_slabs(ref, val):
    rows, d = val.shape
    split = d // LANES
    for a in range(split):
        ref[pl.ds(a, rows, stride=split), :] = val[:, a * LANES:(a + 1) * LANES]


def _load_slabs(ref, rows):
    split = ref.shape[0] // rows
    return jnp.concatenate([ref[pl.ds(a, rows, stride=split), :] for a in range(split)], axis=1)


def _ada_kernel(c_ref, w_ref, b_ref, o_ref, acc_ref, *, tk, tn):
    k = pl.program_id(2)

    @pl.when(k == 0)
    def _():
        acc_ref[...] = jnp.zeros_like(acc_ref)

    cv = c_ref[...]
    sv = cv * _sigmoid(cv)
    s0, s1 = sv[0], sv[1]
    for j in range(tn // LANES):
        cols = slice(j * LANES, (j + 1) * LANES)
        wj = w_ref[0, :, cols]
        acc_ref[0, :, cols] += (wj * s0).reshape(tk // SUBLANES, SUBLANES, LANES).sum(axis=0)
        acc_ref[1, :, cols] += (wj * s1).reshape(tk // SUBLANES, SUBLANES, LANES).sum(axis=0)

    @pl.when(k == pl.num_programs(2) - 1)
    def _():
        o_ref[0] = acc_ref[...].sum(axis=1) + b_ref[0]


def _ada_all(cvec, ada_w, ada_b):
    n_layers, d, n_out = ada_w.shape
    tk = _tile(d, 1024)
    tn = _tile(n_out, 2048, LANES)
    c_rep = jnp.broadcast_to(cvec[:, :, None], (2, d, LANES))
    return pl.pallas_call(
        functools.partial(_ada_kernel, tk=tk, tn=tn),
        grid=(n_layers, n_out // tn, d // tk),
        in_specs=[
            pl.BlockSpec((2, tk, LANES), lambda l, j, k: (0, k, 0)),
            pl.BlockSpec((1, tk, tn), lambda l, j, k: (l, k, j)),
            pl.BlockSpec((1, 1, tn), lambda l, j, k: (l, 0, j)),
        ],
        out_specs=pl.BlockSpec((1, 2, tn), lambda l, j, k: (l, 0, j)),
        out_shape=jax.ShapeDtypeStruct((n_layers, 2, n_out), F32),
        scratch_shapes=[pltpu.VMEM((2, SUBLANES, tn), F32)],
        compiler_params=_params("parallel", "parallel", "arbitrary"),
        name="ada_mod",
    )(c_rep, ada_w, ada_b.reshape(n_layers, 1, n_out))


def _route_rows(lg):
    ng, ne = N_EXPERT_GROUPS, EXPERTS_PER_GROUP
    g = [lg[r:r + 1, :] for r in range(ng)]
    gmax = functools.reduce(jnp.maximum, g)
    gidx = jnp.full(gmax.shape, ng - 1, jnp.int32)
    for r in range(ng - 2, -1, -1):
        gidx = jnp.where(g[r] == gmax, r, gidx)
    p_g = 1.0 / functools.reduce(jnp.add, [jnp.exp(gr - gmax) for gr in g])
    sel = []
    for j in range(ne):
        v = lg[ng + (ng - 1) * ne + j:ng + (ng - 1) * ne + j + 1, :]
        for r in range(ng - 2, -1, -1):
            v = jnp.where(gidx == r, lg[ng + r * ne + j:ng + r * ne + j + 1, :], v)
        sel.append(v)
    v1 = functools.reduce(jnp.maximum, sel)
    i1 = jnp.full(v1.shape, ne - 1, jnp.int32)
    for j in range(ne - 2, -1, -1):
        i1 = jnp.where(sel[j] == v1, j, i1)
    rest = [jnp.where(i1 == j, -jnp.inf, sel[j]) for j in range(ne)]
    v2 = functools.reduce(jnp.maximum, rest)
    i2 = jnp.full(v2.shape, ne - 1, jnp.int32)
    for j in range(ne - 2, -1, -1):
        i2 = jnp.where(rest[j] == v2, j, i2)
    t = jnp.exp(v2 - v1)
    w1 = p_g / (1.0 + t)
    w2 = p_g * t / (1.0 + t)
    ids = jnp.concatenate([gidx * ne + i1, gidx * ne + i2], axis=0)
    wts = jnp.concatenate([w1, w2], axis=0)
    return ids, wts


def _ln_mod_kernel(*refs, n_y, do_ln, out_h, do_route, alpha, slab_y, slab_h):
    refs = list(refs)
    x_ref = refs.pop(0)
    y_refs = [refs.pop(0) for _ in range(n_y)]
    if do_ln:
        gate_ref, lng_ref, lnb_ref = refs.pop(0), refs.pop(0), refs.pop(0)
    if out_h is not None:
        sc_ref, sh_ref = refs.pop(0), refs.pop(0)
    if do_route:
        whi_ref, wlo_ref, rb_ref = refs.pop(0), refs.pop(0), refs.pop(0)
    xo_ref = refs.pop(0) if do_ln else None
    h_ref = refs.pop(0) if out_h is not None else None
    if do_route:
        ids_ref, wts_ref = refs.pop(0), refs.pop(0)
    assert not refs

    x = x_ref[...]
    if do_ln:
        rows = x_ref.shape[0]
        y = functools.reduce(jnp.add, [_load_slabs(r, rows) if slab_y else r[...] for r in y_refs])
        z = alpha * x + gate_ref[0] * y
        mu = jnp.mean(z, axis=-1, keepdims=True)
        zc = z - mu
        var = jnp.mean(zc * zc, axis=-1, keepdims=True)
        x = zc * lax.rsqrt(var + LN_EPS) * lng_ref[...] + lnb_ref[...]
        xo_ref[...] = x
    if out_h is not None:
        h = x * (1.0 + sc_ref[0]) + sh_ref[0]
        if slab_h:
            _store_slabs(h_ref, h.astype(out_h))
        else:
            h_ref[...] = h.astype(out_h)
    if do_route:
        h_hi = h.astype(BF16)
        h_lo = (h - h_hi.astype(F32)).astype(BF16)
        w_hi = whi_ref[...]
        lg = (jnp.dot(h_hi, w_hi, preferred_element_type=F32)
              + jnp.dot(h_lo, w_hi, preferred_element_type=F32)
              + jnp.dot(h_hi, wlo_ref[...], preferred_element_type=F32)) + rb_ref[...]
        ids, wts = _route_rows(lg.T)
        pad = SUBLANES - ids.shape[0]
        ids_ref[...] = jnp.concatenate([ids, jnp.zeros((pad, ids.shape[1]), jnp.int32)], axis=0)
        wts_ref[...] = jnp.concatenate([wts, jnp.zeros((pad, wts.shape[1]), F32)], axis=0)


def _ln_mod(x, ys, gate, ln_g, ln_b, sc, sh, route_w, *, lc, alpha, out_h, row_offset=0,
            slab_y=False, slab_h=False):
    t_all, d = x.shape
    do_ln = len(ys) > 0
    do_route = route_w is not None
    tm = _tile(math.gcd(lc, t_all - lc), 128)
    assert row_offset % tm == 0
    off = row_offset // tm
    t_out = t_all - row_offset
    nct = lc // tm

    row = lambda i: (i + off, 0)
    stream = lambda i: (jnp.where(i + off < nct, 0, 1), 0, 0)
    vec = pl.BlockSpec((1, d), lambda i: (0, 0))
    svec = pl.BlockSpec((1, 1, d), stream)

    args, in_specs = [x], [pl.BlockSpec((tm, d), row)]
    n_y = 0
    yr = d // LANES if slab_y else 1
    for y in ys:
        for grp in range(y.shape[0] // (t_all * yr)):
            n_y += 1
            args.append(y)
            in_specs.append(pl.BlockSpec((tm * yr, d // yr), lambda i, grp=grp: (i + off + grp * (t_all // tm), 0)))
    if do_ln:
        args += [gate, ln_g.reshape(1, d), ln_b.reshape(1, d)]
        in_specs += [svec, vec, vec]
    if out_h is not None:
        args += [sc, sh]
        in_specs += [svec, svec]
    if do_route:
        args += list(route_w)
        in_specs += [pl.BlockSpec((d, LANES), lambda i: (0, 0)),
                     pl.BlockSpec((d, LANES), lambda i: (0, 0)),
                     pl.BlockSpec((1, LANES), lambda i: (0, 0))]
    out_shape, out_specs = [], []
    if do_ln:
        out_shape.append(jax.ShapeDtypeStruct((t_out, d), F32))
        out_specs.append(pl.BlockSpec((tm, d), lambda i: (i, 0)))
    if out_h is not None:
        hr = d // LANES if slab_h else 1
        out_shape.append(jax.ShapeDtypeStruct((t_out * hr, d // hr), out_h))
        out_specs.append(pl.BlockSpec((tm * hr, d // hr), lambda i: (i, 0)))
    if do_route:
        out_shape += [jax.ShapeDtypeStruct((SUBLANES, t_out), jnp.int32),
                      jax.ShapeDtypeStruct((SUBLANES, t_out), F32)]
        out_specs += [pl.BlockSpec((SUBLANES, tm), lambda i: (0, i))] * 2
    return pl.pallas_call(
        functools.partial(_ln_mod_kernel, n_y=n_y, do_ln=do_ln, out_h=out_h,
                          do_route=do_route, alpha=alpha, slab_y=slab_y, slab_h=slab_h),
        grid=(t_out // tm,),
        in_specs=in_specs,
        out_specs=out_specs,
        out_shape=out_shape,
        compiler_params=_params("parallel"),
        name="ln_mod",
    )(*args)


def _proj_kernel(*refs, norm, rope, scale, tn):
    refs = list(refs)
    a_ref, w_ref = refs.pop(0), refs.pop(0)
    g_ref = refs.pop(0) if norm else None
    if rope:
        cos_ref, sin_ref = refs.pop(0), refs.pop(0)
    o_ref = refs.pop(0)
    acc = jnp.dot(a_ref[...], w_ref[...], preferred_element_type=F32)
    if not (norm or rope):
        if scale != 1.0:
            acc = acc * scale
        o_ref[...] = acc.astype(o_ref.dtype)
        return
    if rope:
        cos, sin = cos_ref[...], sin_ref[...]
        lane = lax.broadcasted_iota(jnp.int32, cos.shape, 1)
        first_half = (lane % (HEAD_DIM // 2)) < (HEAD_DIM // 4)
    for hh in range(tn // HEAD_DIM):
        cols = slice(hh * HEAD_DIM, (hh + 1) * HEAD_DIM)
        t = acc[:, cols]
        if norm:
            t = t * lax.rsqrt(jnp.mean(t * t, axis=-1, keepdims=True) + RMS_EPS) * g_ref[...]
        if rope:
            up = pltpu.roll(t, HEAD_DIM - HEAD_DIM // 4, 1)
            dn = pltpu.roll(t, HEAD_DIM // 4, 1)
            t = t * cos + jnp.where(first_half, up, dn) * sin
        if scale != 1.0:
            t = t * scale
        o_ref[:, cols] = t.astype(o_ref.dtype)


def _proj(a, w, layer, *, out_dtype, norm_g=None, rope=None, scale=1.0, tm_target=768, tn_target=1024):
    t_all, kdim = a.shape
    n = w.shape[2]
    tm = _tile(t_all, tm_target)
    tn = _tile(n, tn_target, LANES)
    args = [a, w]
    in_specs = [pl.BlockSpec((tm, kdim), lambda i, j: (i, 0)),
                pl.BlockSpec((None, kdim, tn), lambda i, j: (layer, 0, j))]
    if norm_g is not None:
        args.append(norm_g.reshape(1, HEAD_DIM))
        in_specs.append(pl.BlockSpec((1, HEAD_DIM), lambda i, j: (0, 0)))
    if rope is not None:
        args += list(rope)
        in_specs += [pl.BlockSpec((tm, HEAD_DIM), lambda i, j: (i, 0))] * 2
    return pl.pallas_call(
        functools.partial(_proj_kernel, norm=norm_g is not None, rope=rope is not None,
                          scale=scale, tn=tn),
        grid=(t_all // tm, n // tn),
        in_specs=in_specs,
        out_specs=pl.BlockSpec((tm, tn), lambda i, j: (i, j)),
        out_shape=jax.ShapeDtypeStruct((t_all, n), out_dtype),
        compiler_params=_params("parallel", "parallel"),
        name="proj",
    )(*args)


def _rope_tables(n, lc):
    n_freq = HEAD_DIM // 4
    pos = jnp.arange(n, dtype=jnp.int32)
    row = (pos // GRID_W).astype(F32)
    col = (pos % GRID_W).astype(F32)
    inv_freq = ROPE_THETA ** (-jnp.arange(n_freq, dtype=F32) / n_freq)
    ar, ac = row[:, None] * inv_freq, col[:, None] * inv_freq
    cos = jnp.concatenate([jnp.cos(ar), jnp.cos(ar), jnp.cos(ac), jnp.cos(ac)], axis=1)
    sin = jnp.concatenate([-jnp.sin(ar), jnp.sin(ar), -jnp.sin(ac), jnp.sin(ac)], axis=1)
    cos = jnp.concatenate([jnp.ones((lc, HEAD_DIM), F32), cos], axis=0)
    sin = jnp.concatenate([jnp.zeros((lc, HEAD_DIM), F32), sin], axis=0)
    return cos, sin


def _sweep(qs, kcols, k_ref, v_ref, latent_rows, finish, *, lc, tk, n_lat, dv, sum_on_mxu):
    nq = len(qs)
    n = n_lat // tk

    def e0(size):
        return (lax.broadcasted_iota(jnp.int32, (size, LANES), 1) == 0).astype(v_ref.dtype)

    ones_c, ones_t = (e0(lc), e0(tk)) if sum_on_mxu else (None, None)

    def scores(lo, size):
        return [lax.dot_general(q, k_ref[pl.ds(lo, size), kc], (((1,), (1,)), ((), ())),
                                preferred_element_type=F32) for q, kc in zip(qs, kcols)]

    def pv(p, lo, size, ones):
        vc = v_ref[pl.ds(lo, size), :]
        if sum_on_mxu:
            vc = jnp.concatenate([vc, ones], axis=1)
        return jnp.dot(p, vc, preferred_element_type=F32)

    def soft(s, m, l):
        m_new = jnp.maximum(m, jnp.max(s, axis=-1, keepdims=True))
        a = jnp.exp2(m - m_new)
        p = jnp.exp2(s - m_new)
        if not sum_on_mxu:
            l = a * l + jnp.sum(p, axis=-1, keepdims=True)
        return m_new, a, p.astype(v_ref.dtype), l

    def result(acc, l):
        return acc[:, :dv] / acc[:, dv:dv + 1] if sum_on_mxu else acc / l

    ctx = []
    for s in scores(0, lc):
        m = jnp.max(s, axis=-1, keepdims=True)
        p = jnp.exp2(s - m)
        l = None if sum_on_mxu else jnp.sum(p, axis=-1, keepdims=True)
        ctx.append((m, l, pv(p.astype(v_ref.dtype), 0, lc, ones_c)))

    @pl.when(jnp.logical_not(latent_rows))
    def _():
        finish([result(acc, l) for _, l, acc in ctx])

    @pl.when(latent_rows)
    def _():
        def start(c, s):
            m0, l0, acc0 = ctx[c]
            m, a, p, l = soft(s, m0, l0)
            return (m, p, a * acc0) if sum_on_mxu else (m, l, p, a * acc0)

        state = tuple(start(c, s) for c, s in enumerate(scores(lc, tk)))

        def body(j, state):
            lo = pl.multiple_of(lc + j * tk, math.gcd(lc, tk))
            ss = scores(lo, tk)
            new = []
            for c in range(nq):
                m0, l0 = state[c][0], None if sum_on_mxu else state[c][1]
                acc = state[c][-1] + pv(state[c][-2], lo - tk, tk, ones_t)
                m, a, p, l = soft(ss[c], m0, l0)
                new.append((m, p, a * acc) if sum_on_mxu else (m, l, p, a * acc))
            return tuple(new)

        state = lax.fori_loop(1, n, body, state, unroll=True)
        outs = []
        for c in range(nq):
            l = None if sum_on_mxu else state[c][1]
            p, r = state[c][-2], state[c][-1]
            outs.append(result(r + pv(p, lc + (n - 1) * tk, tk, ones_t), l))
        finish(outs)


def _gqa_kernel(q_ref, k_ref, v_ref, o_ref, *, lc, tq, tk, n_lat):
    i = pl.program_id(1)
    heads = [slice(h * HEAD_DIM, (h + 1) * HEAD_DIM) for h in range(GQA_GROUP)]

    def finish(outs):
        for cols, o in zip(heads, outs):
            o_ref[:, cols] = o.astype(o_ref.dtype)

    _sweep([q_ref[:, cols] for cols in heads], [heads[0]] * GQA_GROUP, k_ref, v_ref, i * tq >= lc, finish,
           lc=lc, tk=tk, n_lat=n_lat, dv=HEAD_DIM, sum_on_mxu=True)


def _gqa_attention(q, k, v, *, lc, tq_target=256, tk_target=2048):
    t_all, dq = q.shape
    n_kv = k.shape[1] // HEAD_DIM
    n_lat = t_all - lc
    tq = _tile(math.gcd(lc, n_lat), tq_target)
    tk = _tile(n_lat, tk_target)
    gw = GQA_GROUP * HEAD_DIM
    return pl.pallas_call(
        functools.partial(_gqa_kernel, lc=lc, tq=tq, tk=tk, n_lat=n_lat),
        grid=(n_kv, t_all // tq),
        in_specs=[pl.BlockSpec((tq, gw), lambda g, i: (i, g)),
                  pl.BlockSpec((t_all, HEAD_DIM), lambda g, i: (0, g)),
                  pl.BlockSpec((t_all, HEAD_DIM), lambda g, i: (0, g))],
        out_specs=pl.BlockSpec((tq, gw), lambda g, i: (i, g)),
        out_shape=jax.ShapeDtypeStruct((t_all, dq), BF16),
        compiler_params=_params("parallel", "parallel"),
        name="gqa_attention",
    )(q, k, v)


def _diff_kernel(q_ref, k_ref, v_ref, lq1_ref, lk1_ref, lq2_ref, lk2_ref, g_ref, o_ref,
                 *, lc, tq, tk, n_lat, lam_init):
    i = pl.program_id(1)
    dv = 2 * HEAD_DIM

    def finish(outs):
        lam = (jnp.exp(jnp.sum(lq1_ref[...] * lk1_ref[...], axis=-1, keepdims=True))
               - jnp.exp(jnp.sum(lq2_ref[...] * lk2_ref[...], axis=-1, keepdims=True)) + lam_init)
        o = outs[0] - lam * outs[1]
        o = o * lax.rsqrt(jnp.mean(o * o, axis=-1, keepdims=True) + RMS_EPS) * g_ref[...]
        o_ref[...] = (o * (1.0 - lam_init)).astype(o_ref.dtype)

    halves = [slice(0, HEAD_DIM), slice(HEAD_DIM, dv)]
    _sweep([q_ref[:, cols] for cols in halves], halves, k_ref, v_ref, i * tq >= lc, finish,
           lc=lc, tk=tk, n_lat=n_lat, dv=dv, sum_on_mxu=False)


def _diff_attention(q, k, v, lq1, lk1, lq2, lk2, subln_g, *, lc, lam_init, tq_target=256, tk_target=1024):
    t_all, dq = q.shape
    dv = 2 * HEAD_DIM
    n_heads = dq // dv
    n_lat = t_all - lc
    tq = _tile(math.gcd(lc, n_lat), tq_target)
    tk = _tile(n_lat, tk_target)
    vec = lambda n: pl.BlockSpec((1, n), lambda h, i: (0, 0))
    return pl.pallas_call(
        functools.partial(_diff_kernel, lc=lc, tq=tq, tk=tk, n_lat=n_lat, lam_init=lam_init),
        grid=(n_heads, t_all // tq),
        in_specs=[pl.BlockSpec((tq, dv), lambda h, i: (i, h)),
                  pl.BlockSpec((t_all, dv), lambda h, i: (0, h)),
                  pl.BlockSpec((t_all, dv), lambda h, i: (0, h)),
                  vec(HEAD_DIM), vec(HEAD_DIM), vec(HEAD_DIM), vec(HEAD_DIM), vec(dv)],
        out_specs=pl.BlockSpec((tq, dv), lambda h, i: (i, h)),
        out_shape=jax.ShapeDtypeStruct((t_all, v.shape[1]), BF16),
        compiler_params=_params("parallel", "parallel"),
        name="diff_attention",
    )(q, k, v, lq1.reshape(1, -1), lk1.reshape(1, -1), lq2.reshape(1, -1), lk2.reshape(1, -1),
      subln_g.reshape(1, -1))


def _pool_kernel(x_ref, xp_ref, xn_ref, sc_ref, sh_ref, w_ref, b_ref, ps_ref, o_ref, buf_ref,
                 *, lc, n_lat, tm):
    i = pl.program_id(0)
    nct = lc // tm
    nt = (lc + n_lat) // tm
    sc, sh = sc_ref[0], sh_ref[0]
    first = jnp.logical_or(i == 0, i == nct)
    last = jnp.logical_or(i == nct - 1, i == nt - 1)
    cur = x_ref[...] * (1.0 + sc) + sh
    buf_ref[0:POOL_HALO, :] = jnp.where(first, 0.0, xp_ref[...] * (1.0 + sc) + sh)
    buf_ref[POOL_HALO:POOL_HALO + tm, :] = cur
    buf_ref[POOL_HALO + tm:2 * POOL_HALO + tm, :] = jnp.where(last, 0.0, xn_ref[...] * (1.0 + sc) + sh)

    in_ctx = i < nct
    t_loc = (lax.broadcasted_iota(jnp.int32, (tm, 1), 0)
             + (i - jnp.where(in_ctx, 0, nct)) * tm)
    n_s = jnp.where(in_ctx, lc, n_lat)
    cdim = w_ref.shape[1]
    for g, win in enumerate(POOL_WINDOWS):
        cols = slice(g * cdim, (g + 1) * cdim)
        half = win // 2
        acc = buf_ref[POOL_HALO - half:POOL_HALO - half + tm, cols]
        for k in range(-half + 1, half):
            acc = acc + buf_ref[POOL_HALO + k:POOL_HALO + k + tm, cols]
        cnt = jnp.minimum(t_loc + half - 1, n_s - 1) - jnp.maximum(t_loc - half, 0) + 1
        y = acc / cnt.astype(F32) - cur[:, cols]
        out = jnp.dot(y.astype(BF16), w_ref[g], preferred_element_type=F32) + b_ref[g]
        o_ref[:, cols] = out * ps_ref[:, cols]


def _pool_mixer(x, sc, sh, pool_w, pool_b, pool_scale, *, lc):
    t_all, d = x.shape
    n_lat = t_all - lc
    ng, cdim, _ = pool_w.shape
    tm = _tile(math.gcd(lc, n_lat), 128)
    hb = tm // POOL_HALO
    n_hb = t_all // POOL_HALO
    stream = lambda i: (jnp.where(i < lc // tm, 0, 1), 0, 0)
    return pl.pallas_call(
        functools.partial(_pool_kernel, lc=lc, n_lat=n_lat, tm=tm),
        grid=(t_all // tm,),
        in_specs=[pl.BlockSpec((tm, d), lambda i: (i, 0)),
                  pl.BlockSpec((POOL_HALO, d), lambda i: (jnp.maximum(i * hb - 1, 0), 0)),
                  pl.BlockSpec((POOL_HALO, d), lambda i: (jnp.minimum((i + 1) * hb, n_hb - 1), 0)),
                  pl.BlockSpec((1, 1, d), stream),
                  pl.BlockSpec((1, 1, d), stream),
                  pl.BlockSpec((ng, cdim, cdim), lambda i: (0, 0, 0)),
                  pl.BlockSpec((ng, 1, cdim), lambda i: (0, 0, 0)),
                  pl.BlockSpec((1, d), lambda i: (0, 0))],
        out_specs=pl.BlockSpec((tm, d), lambda i: (i, 0)),
        out_shape=jax.ShapeDtypeStruct((t_all, d), F32),
        scratch_shapes=[pltpu.VMEM((tm + 2 * POOL_HALO, d), F32)],
        compiler_params=_params("parallel"),
        name="pool_mixer",
    )(x, x, x, sc, sh, pool_w.astype(BF16), pool_b.reshape(ng, 1, cdim), pool_scale.reshape(1, d))


def _moe_kernel(texp_ref, rtok_ref, rdst_ref, nact_ref,
                h_hbm, gate_ref, w1_ref, w3_ref, w2_ref, y_hbm,
                xbuf, ybuf, sem_in, sem_out, *, tm, n_tiles):
    i = pl.program_id(0)
    n = nact_ref[0]
    slot = lax.rem(i, 2)
    split = xbuf.shape[2]
    spare = y_hbm.shape[0] // split - 2 * tm

    def slab(row):
        return pl.ds(pl.multiple_of(row * split, split), split)

    def row_of(buf, s, k):
        return buf.at[s, k // SUBLANES, :, k % SUBLANES, :]

    def gather(tile, s, k):
        return pltpu.make_async_copy(h_hbm.at[slab(rtok_ref[tile * tm + k])], row_of(xbuf, s, k), sem_in.at[s])

    def scatter(tile, s, k):
        return pltpu.make_async_copy(row_of(ybuf, s, k), y_hbm.at[slab(rdst_ref[(tile + 1) * tm + k])],
                                     sem_out.at[s])

    def fill(k):
        return pltpu.make_async_copy(row_of(ybuf, 0, k), y_hbm.at[slab(spare + k)], sem_out.at[0])

    @pl.when(i == 0)
    def _():
        ybuf[...] = jnp.zeros_like(ybuf)
        for k in range(tm):
            fill(k).start()
        for k in range(tm):
            gather(0, 0, k).start()
        for k in range(tm):
            fill(k).wait()

    @pl.when(jnp.logical_and(i >= 1, i < n))
    def _():
        for k in range(tm):
            scatter(i - 2, slot, k).wait()

    @pl.when(i < n)
    def _():
        for k in range(tm):
            gather(i, slot, k).wait()
        nxt = jnp.minimum(i + 1, n_tiles - 1)
        x = jnp.concatenate([xbuf[slot, :, c].reshape(tm, LANES) for c in range(split)], axis=1).astype(BF16)
        for k in range(tm):
            gather(nxt, 1 - slot, k).start()
        a = jnp.dot(x, w1_ref[0], preferred_element_type=F32)
        u = jnp.dot(x, w3_ref[0], preferred_element_type=F32)
        for k in range(tm):
            scatter(i - 1, 1 - slot, k).start()
        hid = a * _sigmoid(a) * u * gate_ref[...]
        y = jnp.dot(hid.astype(BF16), w2_ref[0], preferred_element_type=F32)
        for c in range(split):
            ybuf[slot, :, c] = y[:, c * LANES:(c + 1) * LANES].reshape(tm // SUBLANES, SUBLANES, LANES)

        @pl.when(i == n - 1)
        def _():
            for k in range(tm):
                scatter(i, slot, k).start()
            for k in range(tm):
                gather(nxt, 1 - slot, k).wait()
            for k in range(tm):
                scatter(i - 1, 1 - slot, k).wait()
            for k in range(tm):
                scatter(i, slot, k).wait()


def _moe(h, ids, wts, w1, w3, w2, layer, *, tm_target=256):
    _, n_exp, d, f = w1.shape
    split = d // LANES
    t_all = h.shape[0] // split
    tm = _tile(t_all, tm_target)
    n_pairs = 2 * t_all
    n_tiles = n_pairs // tm + n_exp
    n_rows = n_tiles * tm

    e_flat = ids.reshape(n_pairs)
    w_flat = wts.reshape(n_pairs)
    onehot = (e_flat[:, None] == jnp.arange(n_exp, dtype=jnp.int32)[None, :]).astype(jnp.int32)
    rank = jnp.sum((jnp.cumsum(onehot, axis=0) - onehot) * onehot, axis=1)
    counts = jnp.sum(onehot, axis=0)
    padded = (counts + tm - 1) // tm * tm
    p_end = jnp.cumsum(padded)
    pos = jnp.sum(onehot * (p_end - padded)[None, :], axis=1) + rank
    row_pair = jnp.full((n_rows,), -1, jnp.int32).at[pos].set(jnp.arange(n_pairs, dtype=jnp.int32))
    valid = row_pair >= 0
    safe = jnp.maximum(row_pair, 0)
    row_tok = jnp.where(valid, safe % t_all, 0)
    row_gate = jnp.where(valid, w_flat[safe], 0.0).reshape(n_rows, 1)
    rows = jnp.arange(-tm, n_rows, dtype=jnp.int32)
    spare_row = n_pairs + (rows // tm) % 2 * tm + rows % tm
    row_dst = jnp.where(rows >= 0, jnp.where(jnp.concatenate([jnp.zeros((tm,), bool), valid]),
                                             jnp.concatenate([jnp.zeros((tm,), jnp.int32), row_pair]),
                                             spare_row), spare_row)
    n_act = (p_end[-1] // tm).astype(jnp.int32)
    tile_start = jnp.arange(n_tiles, dtype=jnp.int32) * tm
    tile_exp = jnp.sum(p_end[None, :] <= tile_start[:, None], axis=1, dtype=jnp.int32)
    last_exp = jnp.sum(p_end <= (n_act - 1) * tm, dtype=jnp.int32)
    tile_exp = jnp.minimum(jnp.where(tile_start < n_act * tm, tile_exp, last_exp), n_exp - 1)

    wspec = lambda shape: pl.BlockSpec((None,) + shape, lambda i, te, rt, rd, na: (layer, te[i], 0, 0))
    return pl.pallas_call(
        functools.partial(_moe_kernel, tm=tm, n_tiles=n_tiles),
        grid_spec=pltpu.PrefetchScalarGridSpec(
            num_scalar_prefetch=4,
            grid=(n_tiles,),
            in_specs=[pl.BlockSpec(memory_space=pl.ANY),
                      pl.BlockSpec((tm, 1), lambda i, te, rt, rd, na: (i, 0)),
                      wspec((1, d, f)), wspec((1, d, f)), wspec((1, f, d))],
            out_specs=pl.BlockSpec(memory_space=pl.ANY),
            scratch_shapes=[pltpu.VMEM((2, tm // SUBLANES, split, SUBLANES, LANES), F32)] * 2 + [
                            pltpu.SemaphoreType.DMA((2,)), pltpu.SemaphoreType.DMA((2,))],
        ),
        out_shape=jax.ShapeDtypeStruct(((n_pairs + 2 * tm) * split, LANES), F32),
        compiler_params=_params("arbitrary"),
        name="moe_experts",
    )(tile_exp, row_tok, row_dst, n_act.reshape(1), h, row_gate,
      w1, w3, w2)


def _router_weights(rg_w, rg_b, re_w, re_b):
    d = rg_w.shape[0]
    n = rg_w.shape[1] + re_w.shape[1]
    w = jnp.concatenate([rg_w, re_w, jnp.zeros((d, LANES - n), F32)], axis=1)
    b = jnp.concatenate([rg_b, re_b, jnp.zeros((LANES - n,), F32)]).reshape(1, LANES)
    w_hi = w.astype(BF16)
    w_lo = (w - w_hi.astype(F32)).astype(BF16)
    return w_hi, w_lo, b


def kernel(x, c, ctx, c_ctx, ada_w, ada_b, ln_g, ln_b, attn_wq, attn_wk, attn_wv, attn_wo, attn_qn_g, attn_kn_g, pool_w, pool_b, pool_scale, diff_wq, diff_wk, diff_wv, diff_wo, diff_lq1, diff_lk1, diff_lq2, diff_lk2, diff_subln_g, moe_rg_w, moe_rg_b, moe_re_w, moe_re_b, moe_w1, moe_w3, moe_w2):
    b, n, d = x.shape
    assert b == 1 and c.shape[0] == 1 and ctx.shape[0] == 1
    lc = ctx.shape[1]
    depth = ada_w.shape[0]
    alpha = (2 * depth) ** 0.25
    qk_scale = math.log2(math.e) / math.sqrt(HEAD_DIM)

    s = jnp.concatenate([ctx[0], x[0]], axis=0)
    mods = _ada_all(jnp.concatenate([c_ctx[None, :], c], axis=0), ada_w, ada_b)
    mod = lambda i, k: mods[i, :, k * d:(k + 1) * d].reshape(2, 1, d)
    rope = _rope_tables(n, lc)
    bf = lambda w: w.astype(BF16)
    attn_w = [bf(w) for w in (attn_wq, attn_wk, attn_wv, attn_wo)]
    diff_w = [bf(w) for w in (diff_wq, diff_wk, diff_wv, diff_wo)]
    moe_w = [bf(w) for w in (moe_w1, moe_w3, moe_w2)]

    h1 = None
    for i in range(depth):
        last = i == depth - 1
        kind, j = i % N_MIXERS, i // N_MIXERS
        if kind != 1 and h1 is None:
            (h1,) = _ln_mod(s, (), None, None, None, mod(i, 1), mod(i, 0), None,
                            lc=lc, alpha=alpha, out_h=BF16)
        if kind == 0:
            q = _proj(h1, attn_w[0], j, out_dtype=BF16, norm_g=attn_qn_g[j], rope=rope, scale=qk_scale)
            k = _proj(h1, attn_w[1], j, out_dtype=BF16, norm_g=attn_kn_g[j], rope=rope)
            v = _proj(h1, attn_w[2], j, out_dtype=BF16)
            o = _gqa_attention(q, k, v, lc=lc)
            y = _proj(o, attn_w[3], j, out_dtype=F32)
        elif kind == 1:
            y = _pool_mixer(s, mod(i, 1), mod(i, 0), pool_w[j], pool_b[j], pool_scale[j], lc=lc)
        else:
            lam_init = 0.8 - 0.6 * math.exp(-0.3 * i)
            q = _proj(h1, diff_w[0], j, out_dtype=BF16, rope=rope, scale=qk_scale)
            k = _proj(h1, diff_w[1], j, out_dtype=BF16, rope=rope)
            v = _proj(h1, diff_w[2], j, out_dtype=BF16)
            o = _diff_attention(q, k, v, diff_lq1[j], diff_lk1[j], diff_lq2[j], diff_lk2[j],
                                diff_subln_g[j], lc=lc, lam_init=lam_init)
            y = _proj(o, diff_w[3], j, out_dtype=F32)
        h1 = None
        route_w = _router_weights(moe_rg_w[i], moe_rg_b[i], moe_re_w[i], moe_re_b[i])
        s, h2, ids, wts = _ln_mod(s, (y,), mod(i, 2), ln_g[i, 0], ln_b[i, 0], mod(i, 4), mod(i, 3),
                                  route_w, lc=lc, alpha=alpha, out_h=F32, slab_h=True)
        y2 = _moe(h2, ids[:2], wts[:2], *moe_w, i)
        if last:
            (s,) = _ln_mod(s, (y2,), mod(i, 5), ln_g[i, 1], ln_b[i, 1], None, None, None,
                           lc=lc, alpha=alpha, out_h=None, row_offset=lc, slab_y=True)
        elif (i + 1) % N_MIXERS == 1:
            (s,) = _ln_mod(s, (y2,), mod(i, 5), ln_g[i, 1], ln_b[i, 1], None, None, None,
                           lc=lc, alpha=alpha, out_h=None, slab_y=True)
        else:
            s, h1 = _ln_mod(s, (y2,), mod(i, 5), ln_g[i, 1], ln_b[i, 1], mod(i + 1, 1), mod(i + 1, 0),
                            None, lc=lc, alpha=alpha, out_h=BF16, slab_y=True)
    return s[None]
```

```python
import functools
import math

import jax
import jax.numpy as jnp
from jax import lax
from jax.experimental import pallas as pl
from jax.experimental.pallas import tpu as pltpu

HEAD_DIM = 128
GRID_W = 64
ROPE_THETA = 10000.0
POOL_WINDOWS = (2, 4, 8, 16)
POOL_HALO = 8
N_EXPERT_GROUPS = 4
EXPERTS_PER_GROUP = 4
N_EXPERTS = N_EXPERT_GROUPS * EXPERTS_PER_GROUP
N_MIXERS = 3
GQA_GROUP = 4
LN_EPS = 1e-6
RMS_EPS = 1e-6
LANES = 128
SUBLANES = 8
VMEM_LIMIT_BYTES = 56 * 1024 * 1024

F32 = jnp.float32
BF16 = jnp.bfloat16


def _params(*sem):
    return pltpu.CompilerParams(dimension_semantics=sem, vmem_limit_bytes=VMEM_LIMIT_BYTES)


def _tile(n, target, mult=SUBLANES):
    best = None
    for t in range(mult, min(n, target) + 1, mult):
        if n % t == 0:
            best = t
    assert best is not None, (n, target, mult)
    return best


def _sigmoid(v):
    return 1.0 / (1.0 + jnp.exp(-v))


def _pack_pairs(v):
    half = v.shape[1] // 2
    lo = lax.bitcast_convert_type(v[:, :half].astype(BF16).astype(F32), jnp.uint32)
    hi = lax.bitcast_convert_type(v[:, half:].astype(BF16).astype(F32), jnp.uint32)
    return hi | (lo >> 16)


def _unpack_pairs(u):
    lo = lax.bitcast_convert_type(u << 16, F32)
    hi = lax.bitcast_convert_type(u & jnp.uint32(0xFFFF0000), F32)
    return jnp.concatenate([lo, hi], axis=1)


def _ada_kernel(c_ref, w_ref, b_ref, o_ref, acc_ref, *, tk, tn):
    k = pl.program_id(2)

    @pl.when(k == 0)
    def _():
        acc_ref[...] = jnp.zeros_like(acc_ref)

    cv = c_ref[...]
    sv = cv * _sigmoid(cv)
    s0, s1 = sv[0], sv[1]
    for j in range(tn // LANES):
        cols = slice(j * LANES, (j + 1) * LANES)
        wj = w_ref[0, :, cols]
        acc_ref[0, :, cols] += (wj * s0).reshape(tk // SUBLANES, SUBLANES, LANES).sum(axis=0)
        acc_ref[1, :, cols] += (wj * s1).reshape(tk // SUBLANES, SUBLANES, LANES).sum(axis=0)

    @pl.when(k == pl.num_programs(2) - 1)
    def _():
        o_ref[0] = acc_ref[...].sum(axis=1) + b_ref[0]


def _ada_all(cvec, ada_w, ada_b):
    n_layers, d, n_out = ada_w.shape
    tk = _tile(d, 1024)
    tn = _tile(n_out, 2048, LANES)
    c_rep = jnp.broadcast_to(cvec[:, :, None], (2, d, LANES))
    return pl.pallas_call(
        functools.partial(_ada_kernel, tk=tk, tn=tn),
        grid=(n_layers, n_out // tn, d // tk),
        in_specs=[
            pl.BlockSpec((2, tk, LANES), lambda l, j, k: (0, k, 0)),
            pl.BlockSpec((1, tk, tn), lambda l, j, k: (l, k, j)),
            pl.BlockSpec((1, 1, tn), lambda l, j, k: (l, 0, j)),
        ],
        out_specs=pl.BlockSpec((1, 2, tn), lambda l, j, k: (l, 0, j)),
        out_shape=jax.ShapeDtypeStruct((n_layers, 2, n_out), F32),
        scratch_shapes=[pltpu.VMEM((2, SUBLANES, tn), F32)],
        compiler_params=_params("parallel", "parallel", "arbitrary"),
        name="ada_mod",
    )(c_rep, ada_w, ada_b.reshape(n_layers, 1, n_out))


def _route_rows(lg):
    ng, ne = N_EXPERT_GROUPS, EXPERTS_PER_GROUP
    g = [lg[r:r + 1, :] for r in range(ng)]
    gmax = functools.reduce(jnp.maximum, g)
    gidx = jnp.full(gmax.shape, ng - 1, jnp.int32)
    for r in range(ng - 2, -1, -1):
        gidx = jnp.where(g[r] == gmax, r, gidx)
    p_g = 1.0 / functools.reduce(jnp.add, [jnp.exp(gr - gmax) for gr in g])
    sel = []
    for j in range(ne):
        v = lg[ng + (ng - 1) * ne + j:ng + (ng - 1) * ne + j + 1, :]
        for r in range(ng - 2, -1, -1):
            v = jnp.where(gidx == r, lg[ng + r * ne + j:ng + r * ne + j + 1, :], v)
        sel.append(v)
    v1 = functools.reduce(jnp.maximum, sel)
    i1 = jnp.full(v1.shape, ne - 1, jnp.int32)
    for j in range(ne - 2, -1, -1):
        i1 = jnp.where(sel[j] == v1, j, i1)
    rest = [jnp.where(i1 == j, -jnp.inf, sel[j]) for j in range(ne)]
    v2 = functools.reduce(jnp.maximum, rest)
    i2 = jnp.full(v2.shape, ne - 1, jnp.int32)
    for j in range(ne - 2, -1, -1):
        i2 = jnp.where(rest[j] == v2, j, i2)
    t = jnp.exp(v2 - v1)
    w1 = p_g / (1.0 + t)
    w2 = p_g * t / (1.0 + t)
    ids = jnp.concatenate([gidx * ne + i1, gidx * ne + i2], axis=0)
    wts = jnp.concatenate([w1, w2], axis=0)
    return ids, wts


def _ln_mod_kernel(*refs, n_y, do_ln, out_h, do_route, alpha, packed_y, packed_h):
    refs = list(refs)
    x_ref = refs.pop(0)
    y_refs = [refs.pop(0) for _ in range(n_y)]
    if do_ln:
        gate_ref, lng_ref, lnb_ref = refs.pop(0), refs.pop(0), refs.pop(0)
    if out_h is not None:
        sc_ref, sh_ref = refs.pop(0), refs.pop(0)
    if do_route:
        whi_ref, wlo_ref, rb_ref = refs.pop(0), refs.pop(0), refs.pop(0)
    xo_ref = refs.pop(0) if do_ln else None
    h_ref = refs.pop(0) if out_h is not None else None
    if do_route:
        ids_ref, wts_ref = refs.pop(0), refs.pop(0)
    assert not refs

    x = x_ref[...]
    if do_ln:
        y = functools.reduce(jnp.add, [_unpack_pairs(r[...]) if packed_y else r[...] for r in y_refs])
        z = alpha * x + gate_ref[0] * y
        mu = jnp.mean(z, axis=-1, keepdims=True)
        zc = z - mu
        var = jnp.mean(zc * zc, axis=-1, keepdims=True)
        x = zc * lax.rsqrt(var + LN_EPS) * lng_ref[...] + lnb_ref[...]
        xo_ref[...] = x
    if out_h is not None:
        h = x * (1.0 + sc_ref[0]) + sh_ref[0]
        h_ref[...] = _pack_pairs(h) if packed_h else h.astype(out_h)
    if do_route:
        h_hi = h.astype(BF16)
        h_lo = (h - h_hi.astype(F32)).astype(BF16)
        w_hi = whi_ref[...]
        lg = (jnp.dot(h_hi, w_hi, preferred_element_type=F32)
              + jnp.dot(h_lo, w_hi, preferred_element_type=F32)
              + jnp.dot(h_hi, wlo_ref[...], preferred_element_type=F32)) + rb_ref[...]
        ids, wts = _route_rows(lg.T)
        pad = SUBLANES - ids.shape[0]
        ids_ref[...] = jnp.concatenate([ids, jnp.zeros((pad, ids.shape[1]), jnp.int32)], axis=0)
        wts_ref[...] = jnp.concatenate([wts, jnp.zeros((pad, wts.shape[1]), F32)], axis=0)


def _ln_mod(x, ys, gate, ln_g, ln_b, sc, sh, route_w, *, lc, alpha, out_h, row_offset=0,
            packed_y=False, packed_h=False):
    t_all, d = x.shape
    do_ln = len(ys) > 0
    do_route = route_w is not None
    tm = _tile(math.gcd(lc, t_all - lc), 128)
    assert row_offset % tm == 0
    off = row_offset // tm
    t_out = t_all - row_offset
    nct = lc // tm

    row = lambda i: (i + off, 0)
    stream = lambda i: (jnp.where(i + off < nct, 0, 1), 0, 0)
    vec = pl.BlockSpec((1, d), lambda i: (0, 0))
    svec = pl.BlockSpec((1, 1, d), stream)

    args, in_specs = [x], [pl.BlockSpec((tm, d), row)]
    n_y = 0
    for y in ys:
        for grp in range(y.shape[0] // t_all):
            n_y += 1
            args.append(y)
            in_specs.append(pl.BlockSpec((tm, y.shape[1]), lambda i, grp=grp: (i + off + grp * (t_all // tm), 0)))
    if do_ln:
        args += [gate, ln_g.reshape(1, d), ln_b.reshape(1, d)]
        in_specs += [svec, vec, vec]
    if out_h is not None:
        args += [sc, sh]
        in_specs += [svec, svec]
    if do_route:
        args += list(route_w)
        in_specs += [pl.BlockSpec((d, LANES), lambda i: (0, 0)),
                     pl.BlockSpec((d, LANES), lambda i: (0, 0)),
                     pl.BlockSpec((1, LANES), lambda i: (0, 0))]
    out_shape, out_specs = [], []
    if do_ln:
        out_shape.append(jax.ShapeDtypeStruct((t_out, d), F32))
        out_specs.append(pl.BlockSpec((tm, d), lambda i: (i, 0)))
    if out_h is not None:
        hd, hdt = (d // 2, jnp.uint32) if packed_h else (d, out_h)
        out_shape.append(jax.ShapeDtypeStruct((t_out, hd), hdt))
        out_specs.append(pl.BlockSpec((tm, hd), lambda i: (i, 0)))
    if do_route:
        out_shape += [jax.ShapeDtypeStruct((SUBLANES, t_out), jnp.int32),
                      jax.ShapeDtypeStruct((SUBLANES, t_out), F32)]
        out_specs += [pl.BlockSpec((SUBLANES, tm), lambda i: (0, i))] * 2
    return pl.pallas_call(
        functools.partial(_ln_mod_kernel, n_y=n_y, do_ln=do_ln, out_h=out_h,
                          do_route=do_route, alpha=alpha, packed_y=packed_y, packed_h=packed_h),
        grid=(t_out // tm,),
        in_specs=in_specs,
        out_specs=out_specs,
        out_shape=out_shape,
        compiler_params=_params("parallel"),
        name="ln_mod",
    )(*args)


def _proj_kernel(*refs, norm, rope, scale, tn):
    refs = list(refs)
    a_ref, w_ref = refs.pop(0), refs.pop(0)
    g_ref = refs.pop(0) if norm else None
    if rope:
        cos_ref, sin_ref = refs.pop(0), refs.pop(0)
    o_ref = refs.pop(0)
    acc = jnp.dot(a_ref[...], w_ref[...], preferred_element_type=F32)
    if not (norm or rope):
        if scale != 1.0:
            acc = acc * scale
        o_ref[...] = acc.astype(o_ref.dtype)
        return
    if rope:
        cos, sin = cos_ref[...], sin_ref[...]
        lane = lax.broadcasted_iota(jnp.int32, cos.shape, 1)
        first_half = (lane % (HEAD_DIM // 2)) < (HEAD_DIM // 4)
    for hh in range(tn // HEAD_DIM):
        cols = slice(hh * HEAD_DIM, (hh + 1) * HEAD_DIM)
        t = acc[:, cols]
        if norm:
            t = t * lax.rsqrt(jnp.mean(t * t, axis=-1, keepdims=True) + RMS_EPS) * g_ref[...]
        if rope:
            up = pltpu.roll(t, HEAD_DIM - HEAD_DIM // 4, 1)
            dn = pltpu.roll(t, HEAD_DIM // 4, 1)
            t = t * cos + jnp.where(first_half, up, dn) * sin
        if scale != 1.0:
            t = t * scale
        o_ref[:, cols] = t.astype(o_ref.dtype)


def _proj(a, w, layer, *, out_dtype, norm_g=None, rope=None, scale=1.0, tm_target=768, tn_target=1024):
    t_all, kdim = a.shape
    n = w.shape[2]
    tm = _tile(t_all, tm_target)
    tn = _tile(n, tn_target, LANES)
    args = [a, w]
    in_specs = [pl.BlockSpec((tm, kdim), lambda i, j: (i, 0)),
                pl.BlockSpec((None, kdim, tn), lambda i, j: (layer, 0, j))]
    if norm_g is not None:
        args.append(norm_g.reshape(1, HEAD_DIM))
        in_specs.append(pl.BlockSpec((1, HEAD_DIM), lambda i, j: (0, 0)))
    if rope is not None:
        args += list(rope)
        in_specs += [pl.BlockSpec((tm, HEAD_DIM), lambda i, j: (i, 0))] * 2
    return pl.pallas_call(
        functools.partial(_proj_kernel, norm=norm_g is not None, rope=rope is not None,
                          scale=scale, tn=tn),
        grid=(t_all // tm, n // tn),
        in_specs=in_specs,
        out_specs=pl.BlockSpec((tm, tn), lambda i, j: (i, j)),
        out_shape=jax.ShapeDtypeStruct((t_all, n), out_dtype),
        compiler_params=_params("parallel", "parallel"),
        name="proj",
    )(*args)


def _rope_tables(n, lc):
    n_freq = HEAD_DIM // 4
    pos = jnp.arange(n, dtype=jnp.int32)
    row = (pos // GRID_W).astype(F32)
    col = (pos % GRID_W).astype(F32)
    inv_freq = ROPE_THETA ** (-jnp.arange(n_freq, dtype=F32) / n_freq)
    ar, ac = row[:, None] * inv_freq, col[:, None] * inv_freq
    cos = jnp.concatenate([jnp.cos(ar), jnp.cos(ar), jnp.cos(ac), jnp.cos(ac)], axis=1)
    sin = jnp.concatenate([-jnp.sin(ar), jnp.sin(ar), -jnp.sin(ac), jnp.sin(ac)], axis=1)
    cos = jnp.concatenate([jnp.ones((lc, HEAD_DIM), F32), cos], axis=0)
    sin = jnp.concatenate([jnp.zeros((lc, HEAD_DIM), F32), sin], axis=0)
    return cos, sin


def _sweep(qs, kcols, k_ref, v_ref, latent_rows, finish, *, lc, tk, n_lat, dv, sum_on_mxu):
    nq = len(qs)
    n = n_lat // tk

    def e0(size):
        return (lax.broadcasted_iota(jnp.int32, (size, LANES), 1) == 0).astype(v_ref.dtype)

    ones_c, ones_t = (e0(lc), e0(tk)) if sum_on_mxu else (None, None)

    def scores(lo, size):
        return [lax.dot_general(q, k_ref[pl.ds(lo, size), kc], (((1,), (1,)), ((), ())),
                                preferred_element_type=F32) for q, kc in zip(qs, kcols)]

    def pv(p, lo, size, ones):
        vc = v_ref[pl.ds(lo, size), :]
        if sum_on_mxu:
            vc = jnp.concatenate([vc, ones], axis=1)
        return jnp.dot(p, vc, preferred_element_type=F32)

    def soft(s, m, l):
        m_new = jnp.maximum(m, jnp.max(s, axis=-1, keepdims=True))
        a = jnp.exp2(m - m_new)
        p = jnp.exp2(s - m_new)
        if not sum_on_mxu:
            l = a * l + jnp.sum(p, axis=-1, keepdims=True)
        return m_new, a, p.astype(v_ref.dtype), l

    def result(acc, l):
        return acc[:, :dv] / acc[:, dv:dv + 1] if sum_on_mxu else acc / l

    ctx = []
    for s in scores(0, lc):
        m = jnp.max(s, axis=-1, keepdims=True)
        p = jnp.exp2(s - m)
        l = None if sum_on_mxu else jnp.sum(p, axis=-1, keepdims=True)
        ctx.append((m, l, pv(p.astype(v_ref.dtype), 0, lc, ones_c)))

    @pl.when(jnp.logical_not(latent_rows))
    def _():
        finish([result(acc, l) for _, l, acc in ctx])

    @pl.when(latent_rows)
    def _():
        def start(c, s):
            m0, l0, acc0 = ctx[c]
            m, a, p, l = soft(s, m0, l0)
            return (m, p, a * acc0) if sum_on_mxu else (m, l, p, a * acc0)

        state = tuple(start(c, s) for c, s in enumerate(scores(lc, tk)))

        def body(j, state):
            lo = pl.multiple_of(lc + j * tk, math.gcd(lc, tk))
            ss = scores(lo, tk)
            new = []
            for c in range(nq):
                m0, l0 = state[c][0], None if sum_on_mxu else state[c][1]
                acc = state[c][-1] + pv(state[c][-2], lo - tk, tk, ones_t)
                m, a, p, l = soft(ss[c], m0, l0)
                new.append((m, p, a * acc) if sum_on_mxu else (m, l, p, a * acc))
            return tuple(new)

        state = lax.fori_loop(1, n, body, state, unroll=True)
        outs = []
        for c in range(nq):
            l = None if sum_on_mxu else state[c][1]
            p, r = state[c][-2], state[c][-1]
            outs.append(result(r + pv(p, lc + (n - 1) * tk, tk, ones_t), l))
        finish(outs)


def _gqa_kernel(q_ref, k_ref, v_ref, o_ref, *, lc, tq, tk, n_lat):
    i = pl.program_id(1)
    heads = [slice(h * HEAD_DIM, (h + 1) * HEAD_DIM) for h in range(GQA_GROUP)]

    def finish(outs):
        for cols, o in zip(heads, outs):
            o_ref[:, cols] = o.astype(o_ref.dtype)

    _sweep([q_ref[:, cols] for cols in heads], [heads[0]] * GQA_GROUP, k_ref, v_ref, i * tq >= lc, finish,
           lc=lc, tk=tk, n_lat=n_lat, dv=HEAD_DIM, sum_on_mxu=True)


def _gqa_attention(q, k, v, *, lc, tq_target=256, tk_target=2048):
    t_all, dq = q.shape
    n_kv = k.shape[1] // HEAD_DIM
    n_lat = t_all - lc
    tq = _tile(math.gcd(lc, n_lat), tq_target)
    tk = _tile(n_lat, tk_target)
    gw = GQA_GROUP * HEAD_DIM
    return pl.pallas_call(
        functools.partial(_gqa_kernel, lc=lc, tq=tq, tk=tk, n_lat=n_lat),
        grid=(n_kv, t_all // tq),
        in_specs=[pl.BlockSpec((tq, gw), lambda g, i: (i, g)),
                  pl.BlockSpec((t_all, HEAD_DIM), lambda g, i: (0, g)),
                  pl.BlockSpec((t_all, HEAD_DIM), lambda g, i: (0, g))],
        out_specs=pl.BlockSpec((tq, gw), lambda g, i: (i, g)),
        out_shape=jax.ShapeDtypeStruct((t_all, dq), BF16),
        compiler_params=_params("parallel", "parallel"),
        name="gqa_attention",
    )(q, k, v)


def _diff_kernel(q_ref, k_ref, v_ref, lq1_ref, lk1_ref, lq2_ref, lk2_ref, g_ref, o_ref,
                 *, lc, tq, tk, n_lat, lam_init):
    i = pl.program_id(1)
    dv = 2 * HEAD_DIM

    def finish(outs):
        lam = (jnp.exp(jnp.sum(lq1_ref[...] * lk1_ref[...], axis=-1, keepdims=True))
               - jnp.exp(jnp.sum(lq2_ref[...] * lk2_ref[...], axis=-1, keepdims=True)) + lam_init)
        o = outs[0] - lam * outs[1]
        o = o * lax.rsqrt(jnp.mean(o * o, axis=-1, keepdims=True) + RMS_EPS) * g_ref[...]
        o_ref[...] = (o * (1.0 - lam_init)).astype(o_ref.dtype)

    halves = [slice(0, HEAD_DIM), slice(HEAD_DIM, dv)]
    _sweep([q_ref[:, cols] for cols in halves], halves, k_ref, v_ref, i * tq >= lc, finish,
           lc=lc, tk=tk, n_lat=n_lat, dv=dv, sum_on_mxu=False)


def _diff_attention(q, k, v, lq1, lk1, lq2, lk2, subln_g, *, lc, lam_init, tq_target=256, tk_target=1024):
    t_all, dq = q.shape
    dv = 2 * HEAD_DIM
    n_heads = dq // dv
    n_lat = t_all - lc
    tq = _tile(math.gcd(lc, n_lat), tq_target)
    tk = _tile(n_lat, tk_target)
    vec = lambda n: pl.BlockSpec((1, n), lambda h, i: (0, 0))
    return pl.pallas_call(
        functools.partial(_diff_kernel, lc=lc, tq=tq, tk=tk, n_lat=n_lat, lam_init=lam_init),
        grid=(n_heads, t_all // tq),
        in_specs=[pl.BlockSpec((tq, dv), lambda h, i: (i, h)),
                  pl.BlockSpec((t_all, dv), lambda h, i: (0, h)),
                  pl.BlockSpec((t_all, dv), lambda h, i: (0, h)),
                  vec(HEAD_DIM), vec(HEAD_DIM), vec(HEAD_DIM), vec(HEAD_DIM), vec(dv)],
        out_specs=pl.BlockSpec((tq, dv), lambda h, i: (i, h)),
        out_shape=jax.ShapeDtypeStruct((t_all, v.shape[1]), BF16),
        compiler_params=_params("parallel", "parallel"),
        name="diff_attention",
    )(q, k, v, lq1.reshape(1, -1), lk1.reshape(1, -1), lq2.reshape(1, -1), lk2.reshape(1, -1),
      subln_g.reshape(1, -1))


def _pool_kernel(x_ref, xp_ref, xn_ref, sc_ref, sh_ref, w_ref, b_ref, ps_ref, o_ref, buf_ref,
                 *, lc, n_lat, tm):
    i = pl.program_id(0)
    nct = lc // tm
    nt = (lc + n_lat) // tm
    sc, sh = sc_ref[0], sh_ref[0]
    first = jnp.logical_or(i == 0, i == nct)
    last = jnp.logical_or(i == nct - 1, i == nt - 1)
    cur = x_ref[...] * (1.0 + sc) + sh
    buf_ref[0:POOL_HALO, :] = jnp.where(first, 0.0, xp_ref[...] * (1.0 + sc) + sh)
    buf_ref[POOL_HALO:POOL_HALO + tm, :] = cur
    buf_ref[POOL_HALO + tm:2 * POOL_HALO + tm, :] = jnp.where(last, 0.0, xn_ref[...] * (1.0 + sc) + sh)

    in_ctx = i < nct
    t_loc = (lax.broadcasted_iota(jnp.int32, (tm, 1), 0)
             + (i - jnp.where(in_ctx, 0, nct)) * tm)
    n_s = jnp.where(in_ctx, lc, n_lat)
    cdim = w_ref.shape[1]
    for g, win in enumerate(POOL_WINDOWS):
        cols = slice(g * cdim, (g + 1) * cdim)
        half = win // 2
        acc = buf_ref[POOL_HALO - half:POOL_HALO - half + tm, cols]
        for k in range(-half + 1, half):
            acc = acc + buf_ref[POOL_HALO + k:POOL_HALO + k + tm, cols]
        cnt = jnp.minimum(t_loc + half - 1, n_s - 1) - jnp.maximum(t_loc - half, 0) + 1
        y = acc / cnt.astype(F32) - cur[:, cols]
        out = jnp.dot(y.astype(BF16), w_ref[g], preferred_element_type=F32) + b_ref[g]
        o_ref[:, cols] = out * ps_ref[:, cols]


def _pool_mixer(x, sc, sh, pool_w, pool_b, pool_scale, *, lc):
    t_all, d = x.shape
    n_lat = t_all - lc
    ng, cdim, _ = pool_w.shape
    tm = _tile(math.gcd(lc, n_lat), 128)
    hb = tm // POOL_HALO
    n_hb = t_all // POOL_HALO
    stream = lambda i: (jnp.where(i < lc // tm, 0, 1), 0, 0)
    return pl.pallas_call(
        functools.partial(_pool_kernel, lc=lc, n_lat=n_lat, tm=tm),
        grid=(t_all // tm,),
        in_specs=[pl.BlockSpec((tm, d), lambda i: (i, 0)),
                  pl.BlockSpec((POOL_HALO, d), lambda i: (jnp.maximum(i * hb - 1, 0), 0)),
                  pl.BlockSpec((POOL_HALO, d), lambda i: (jnp.minimum((i + 1) * hb, n_hb - 1), 0)),
                  pl.BlockSpec((1, 1, d), stream),
                  pl.BlockSpec((1, 1, d), stream),
                  pl.BlockSpec((ng, cdim, cdim), lambda i: (0, 0, 0)),
                  pl.BlockSpec((ng, 1, cdim), lambda i: (0, 0, 0)),
                  pl.BlockSpec((1, d), lambda i: (0, 0))],
        out_specs=pl.BlockSpec((tm, d), lambda i: (i, 0)),
        out_shape=jax.ShapeDtypeStruct((t_all, d), F32),
        scratch_shapes=[pltpu.VMEM((tm + 2 * POOL_HALO, d), F32)],
        compiler_params=_params("parallel"),
        name="pool_mixer",
    )(x, x, x, sc, sh, pool_w.astype(BF16), pool_b.reshape(ng, 1, cdim), pool_scale.reshape(1, d))


def _moe_kernel(texp_ref, rtok_ref, rdst_ref, nact_ref,
                h_hbm, gate_ref, w1_ref, w3_ref, w2_ref, y_hbm,
                xbuf, ybuf, sem_in, sem_out, *, tm, n_tiles):
    i = pl.program_id(0)
    n = nact_ref[0]
    slot = lax.rem(i, 2)
    spare = y_hbm.shape[0] - 2 * tm

    def gather(tile, s, k):
        tok = rtok_ref[tile * tm + k]
        return pltpu.make_async_copy(h_hbm.at[pl.ds(tok, 1)], xbuf.at[s, pl.ds(k, 1)], sem_in.at[s])

    def scatter(tile, s, k):
        dst = rdst_ref[(tile + 1) * tm + k]
        return pltpu.make_async_copy(ybuf.at[s, pl.ds(k, 1)], y_hbm.at[pl.ds(dst, 1)], sem_out.at[s])

    def fill(k):
        return pltpu.make_async_copy(ybuf.at[0, pl.ds(k, 1)], y_hbm.at[pl.ds(spare + k, 1)], sem_out.at[0])

    @pl.when(i == 0)
    def _():
        ybuf[...] = jnp.zeros_like(ybuf)
        for k in range(tm):
            fill(k).start()
        for k in range(tm):
            gather(0, 0, k).start()
        for k in range(tm):
            fill(k).wait()

    @pl.when(jnp.logical_and(i >= 1, i < n))
    def _():
        for k in range(tm):
            scatter(i - 2, slot, k).wait()

    @pl.when(i < n)
    def _():
        for k in range(tm):
            gather(i, slot, k).wait()
        nxt = jnp.minimum(i + 1, n_tiles - 1)
        x = _unpack_pairs(xbuf[slot]).astype(BF16)
        for k in range(tm):
            gather(nxt, 1 - slot, k).start()
        a = jnp.dot(x, w1_ref[0], preferred_element_type=F32)
        u = jnp.dot(x, w3_ref[0], preferred_element_type=F32)
        for k in range(tm):
            scatter(i - 1, 1 - slot, k).start()
        hid = a * _sigmoid(a) * u * gate_ref[...]
        ybuf[slot] = _pack_pairs(jnp.dot(hid.astype(BF16), w2_ref[0], preferred_element_type=F32))

        @pl.when(i == n - 1)
        def _():
            for k in range(tm):
                scatter(i, slot, k).start()
            for k in range(tm):
                gather(nxt, 1 - slot, k).wait()
            for k in range(tm):
                scatter(i - 1, 1 - slot, k).wait()
            for k in range(tm):
                scatter(i, slot, k).wait()


def _moe(h, ids, wts, w1, w3, w2, layer, *, tm_target=256):
    _, n_exp, d, f = w1.shape
    t_all = h.shape[0]
    tm = _tile(t_all, tm_target)
    n_pairs = 2 * t_all
    n_tiles = n_pairs // tm + n_exp
    n_rows = n_tiles * tm

    e_flat = ids.reshape(n_pairs)
    w_flat = wts.reshape(n_pairs)
    onehot = (e_flat[:, None] == jnp.arange(n_exp, dtype=jnp.int32)[None, :]).astype(jnp.int32)
    rank = jnp.sum((jnp.cumsum(onehot, axis=0) - onehot) * onehot, axis=1)
    counts = jnp.sum(onehot, axis=0)
    padded = (counts + tm - 1) // tm * tm
    p_end = jnp.cumsum(padded)
    pos = jnp.sum(onehot * (p_end - padded)[None, :], axis=1) + rank
    row_pair = jnp.full((n_rows,), -1, jnp.int32).at[pos].set(jnp.arange(n_pairs, dtype=jnp.int32))
    valid = row_pair >= 0
    safe = jnp.maximum(row_pair, 0)
    row_tok = jnp.where(valid, safe % t_all, 0)
    row_gate = jnp.where(valid, w_flat[safe], 0.0).reshape(n_rows, 1)
    rows = jnp.arange(-tm, n_rows, dtype=jnp.int32)
    spare_row = n_pairs + (rows // tm) % 2 * tm + rows % tm
    row_dst = jnp.where(rows >= 0, jnp.where(jnp.concatenate([jnp.zeros((tm,), bool), valid]),
                                             jnp.concatenate([jnp.zeros((tm,), jnp.int32), row_pair]),
                                             spare_row), spare_row)
    n_act = (p_end[-1] // tm).astype(jnp.int32)
    tile_start = jnp.arange(n_tiles, dtype=jnp.int32) * tm
    tile_exp = jnp.sum(p_end[None, :] <= tile_start[:, None], axis=1, dtype=jnp.int32)
    last_exp = jnp.sum(p_end <= (n_act - 1) * tm, dtype=jnp.int32)
    tile_exp = jnp.minimum(jnp.where(tile_start < n_act * tm, tile_exp, last_exp), n_exp - 1)

    wspec = lambda shape: pl.BlockSpec((None,) + shape, lambda i, te, rt, rd, na: (layer, te[i], 0, 0))
    return pl.pallas_call(
        functools.partial(_moe_kernel, tm=tm, n_tiles=n_tiles),
        grid_spec=pltpu.PrefetchScalarGridSpec(
            num_scalar_prefetch=4,
            grid=(n_tiles,),
            in_specs=[pl.BlockSpec(memory_space=pl.ANY),
                      pl.BlockSpec((tm, 1), lambda i, te, rt, rd, na: (i, 0)),
                      wspec((1, d, f)), wspec((1, d, f)), wspec((1, f, d))],
            out_specs=pl.BlockSpec(memory_space=pl.ANY),
            scratch_shapes=[pltpu.VMEM((2, tm, d // 2), jnp.uint32)] * 2 + [
                            pltpu.SemaphoreType.DMA((2,)), pltpu.SemaphoreType.DMA((2,))],
        ),
        out_shape=jax.ShapeDtypeStruct((n_pairs + 2 * tm, d // 2), jnp.uint32),
        compiler_params=_params("arbitrary"),
        name="moe_experts",
    )(tile_exp, row_tok, row_dst, n_act.reshape(1), h, row_gate,
      w1, w3, w2)


def _router_weights(rg_w, rg_b, re_w, re_b):
    d = rg_w.shape[0]
    n = rg_w.shape[1] + re_w.shape[1]
    w = jnp.concatenate([rg_w, re_w, jnp.zeros((d, LANES - n), F32)], axis=1)
    b = jnp.concatenate([rg_b, re_b, jnp.zeros((LANES - n,), F32)]).reshape(1, LANES)
    w_hi = w.astype(BF16)
    w_lo = (w - w_hi.astype(F32)).astype(BF16)
    return w_hi, w_lo, b


def kernel(x, c, ctx, c_ctx, ada_w, ada_b, ln_g, ln_b, attn_wq, attn_wk, attn_wv, attn_wo, attn_qn_g, attn_kn_g, pool_w, pool_b, pool_scale, diff_wq, diff_wk, diff_wv, diff_wo, diff_lq1, diff_lk1, diff_lq2, diff_lk2, diff_subln_g, moe_rg_w, moe_rg_b, moe_re_w, moe_re_b, moe_w1, moe_w3, moe_w2):
    b, n, d = x.shape
    assert b == 1 and c.shape[0] == 1 and ctx.shape[0] == 1
    lc = ctx.shape[1]
    depth = ada_w.shape[0]
    alpha = (2 * depth) ** 0.25
    qk_scale = math.log2(math.e) / math.sqrt(HEAD_DIM)

    s = jnp.concatenate([ctx[0], x[0]], axis=0)
    mods = _ada_all(jnp.concatenate([c_ctx[None, :], c], axis=0), ada_w, ada_b)
    mod = lambda i, k: mods[i, :, k * d:(k + 1) * d].reshape(2, 1, d)
    rope = _rope_tables(n, lc)
    bf = lambda w: w.astype(BF16)
    attn_w = [bf(w) for w in (attn_wq, attn_wk, attn_wv, attn_wo)]
    diff_w = [bf(w) for w in (diff_wq, diff_wk, diff_wv, diff_wo)]
    moe_w = [bf(w) for w in (moe_w1, moe_w3, moe_w2)]

    h1 = None
    for i in range(depth):
        last = i == depth - 1
        kind, j = i % N_MIXERS, i // N_MIXERS
        if kind != 1 and h1 is None:
            (h1,) = _ln_mod(s, (), None, None, None, mod(i, 1), mod(i, 0), None,
                            lc=lc, alpha=alpha, out_h=BF16)
        if kind == 0:
            q = _proj(h1, attn_w[0], j, out_dtype=BF16, norm_g=attn_qn_g[j], rope=rope, scale=qk_scale)
            k = _proj(h1, attn_w[1], j, out_dtype=BF16, norm_g=attn_kn_g[j], rope=rope)
            v = _proj(h1, attn_w[2], j, out_dtype=BF16)
            o = _gqa_attention(q, k, v, lc=lc)
            y = _proj(o, attn_w[3], j, out_dtype=F32)
        elif kind == 1:
            y = _pool_mixer(s, mod(i, 1), mod(i, 0), pool_w[j], pool_b[j], pool_scale[j], lc=lc)
        else:
            lam_init = 0.8 - 0.6 * math.exp(-0.3 * i)
            q = _proj(h1, diff_w[0], j, out_dtype=BF16, rope=rope, scale=qk_scale)
            k = _proj(h1, diff_w[1], j, out_dtype=BF16, rope=rope)
            v = _proj(h1, diff_w[2], j, out_dtype=BF16)
            o = _diff_attention(q, k, v, diff_lq1[j], diff_lk1[j], diff_lq2[j], diff_lk2[j],
                                diff_subln_g[j], lc=lc, lam_init=lam_init)
            y = _proj(o, diff_w[3], j, out_dtype=F32)
        h1 = None
        route_w = _router_weights(moe_rg_w[i], moe_rg_b[i], moe_re_w[i], moe_re_b[i])
        s, h2, ids, wts = _ln_mod(s, (y,), mod(i, 2), ln_g[i, 0], ln_b[i, 0], mod(i, 4), mod(i, 3),
                                  route_w, lc=lc, alpha=alpha, out_h=F32, packed_h=True)
        y2 = _moe(h2, ids[:2], wts[:2], *moe_w, i)
        if last:
            (s,) = _ln_mod(s, (y2,), mod(i, 5), ln_g[i, 1], ln_b[i, 1], None, None, None,
                           lc=lc, alpha=alpha, out_h=None, row_offset=lc, packed_y=True)
        elif (i + 1) % N_MIXERS == 1:
            (s,) = _ln_mod(s, (y2,), mod(i, 5), ln_g[i, 1], ln_b[i, 1], None, None, None,
                           lc=lc, alpha=alpha, out_h=None, packed_y=True)
        else:
            s, h1 = _ln_mod(s, (y2,), mod(i, 5), ln_g[i, 1], ln_b[i, 1], mod(i + 1, 1), mod(i + 1, 0),
                            None, lc=lc, alpha=alpha, out_h=BF16, packed_y=True)
    return s[None]
```

```python
import functools
import math

import jax
import jax.numpy as jnp
from jax import lax
from jax.experimental import pallas as pl
from jax.experimental.pallas import tpu as pltpu

HEAD_DIM = 128
GRID_W = 64
ROPE_THETA = 10000.0
POOL_WINDOWS = (2, 4, 8, 16)
POOL_HALO = 8
N_EXPERT_GROUPS = 4
EXPERTS_PER_GROUP = 4
N_EXPERTS = N_EXPERT_GROUPS * EXPERTS_PER_GROUP
N_MIXERS = 3
GQA_GROUP = 4
LN_EPS = 1e-6
RMS_EPS = 1e-6
LANES = 128
SUBLANES = 8
VMEM_LIMIT_BYTES = 56 * 1024 * 1024

F32 = jnp.float32
BF16 = jnp.bfloat16


def _params(*sem):
    return pltpu.CompilerParams(dimension_semantics=sem, vmem_limit_bytes=VMEM_LIMIT_BYTES)


def _tile(n, target, mult=SUBLANES):
    best = None
    for t in range(mult, min(n, target) + 1, mult):
        if n % t == 0:
            best = t
    assert best is not None, (n, target, mult)
    return best


def _sigmoid(v):
    return 1.0 / (1.0 + jnp.exp(-v))


def _pack_pairs(v):
    half = v.shape[1] // 2
    lo = lax.bitcast_convert_type(v[:, :half].astype(BF16).astype(F32), jnp.uint32)
    hi = lax.bitcast_convert_type(v[:, half:].astype(BF16).astype(F32), jnp.uint32)
    return hi | (lo >> 16)


def _unpack_pairs(u):
    lo = lax.bitcast_convert_type(u << 16, F32)
    hi = lax.bitcast_convert_type(u & jnp.uint32(0xFFFF0000), F32)
    return jnp.concatenate([lo, hi], axis=1)


def _ada_kernel(c_ref, w_ref, b_ref, o_ref, acc_ref, *, tk, tn):
    k = pl.program_id(2)

    @pl.when(k == 0)
    def _():
        acc_ref[...] = jnp.zeros_like(acc_ref)

    cv = c_ref[...]
    sv = cv * _sigmoid(cv)
    s0, s1 = sv[0], sv[1]
    for j in range(tn // LANES):
        cols = slice(j * LANES, (j + 1) * LANES)
        wj = w_ref[0, :, cols]
        acc_ref[0, :, cols] += (wj * s0).reshape(tk // SUBLANES, SUBLANES, LANES).sum(axis=0)
        acc_ref[1, :, cols] += (wj * s1).reshape(tk // SUBLANES, SUBLANES, LANES).sum(axis=0)

    @pl.when(k == pl.num_programs(2) - 1)
    def _():
        o_ref[0] = acc_ref[...].sum(axis=1) + b_ref[0]


def _ada_all(cvec, ada_w, ada_b):
    n_layers, d, n_out = ada_w.shape
    tk = _tile(d, 1024)
    tn = _tile(n_out, 2048, LANES)
    c_rep = jnp.broadcast_to(cvec[:, :, None], (2, d, LANES))
    return pl.pallas_call(
        functools.partial(_ada_kernel, tk=tk, tn=tn),
        grid=(n_layers, n_out // tn, d // tk),
        in_specs=[
            pl.BlockSpec((2, tk, LANES), lambda l, j, k: (0, k, 0)),
            pl.BlockSpec((1, tk, tn), lambda l, j, k: (l, k, j)),
            pl.BlockSpec((1, 1, tn), lambda l, j, k: (l, 0, j)),
        ],
        out_specs=pl.BlockSpec((1, 2, tn), lambda l, j, k: (l, 0, j)),
        out_shape=jax.ShapeDtypeStruct((n_layers, 2, n_out), F32),
        scratch_shapes=[pltpu.VMEM((2, SUBLANES, tn), F32)],
        compiler_params=_params("parallel", "parallel", "arbitrary"),
        name="ada_mod",
    )(c_rep, ada_w, ada_b.reshape(n_layers, 1, n_out))


def _route_rows(lg):
    ng, ne = N_EXPERT_GROUPS, EXPERTS_PER_GROUP
    g = [lg[r:r + 1, :] for r in range(ng)]
    gmax = functools.reduce(jnp.maximum, g)
    gidx = jnp.full(gmax.shape, ng - 1, jnp.int32)
    for r in range(ng - 2, -1, -1):
        gidx = jnp.where(g[r] == gmax, r, gidx)
    p_g = 1.0 / functools.reduce(jnp.add, [jnp.exp(gr - gmax) for gr in g])
    sel = []
    for j in range(ne):
        v = lg[ng + (ng - 1) * ne + j:ng + (ng - 1) * ne + j + 1, :]
        for r in range(ng - 2, -1, -1):
            v = jnp.where(gidx == r, lg[ng + r * ne + j:ng + r * ne + j + 1, :], v)
        sel.append(v)
    v1 = functools.reduce(jnp.maximum, sel)
    i1 = jnp.full(v1.shape, ne - 1, jnp.int32)
    for j in range(ne - 2, -1, -1):
        i1 = jnp.where(sel[j] == v1, j, i1)
    rest = [jnp.where(i1 == j, -jnp.inf, sel[j]) for j in range(ne)]
    v2 = functools.reduce(jnp.maximum, rest)
    i2 = jnp.full(v2.shape, ne - 1, jnp.int32)
    for j in range(ne - 2, -1, -1):
        i2 = jnp.where(rest[j] == v2, j, i2)
    t = jnp.exp(v2 - v1)
    w1 = p_g / (1.0 + t)
    w2 = p_g * t / (1.0 + t)
    ids = jnp.concatenate([gidx * ne + i1, gidx * ne + i2], axis=0)
    wts = jnp.concatenate([w1, w2], axis=0)
    return ids, wts


def _ln_mod_kernel(*refs, n_y, do_ln, out_h, do_route, alpha, packed_y, packed_h):
    refs = list(refs)
    x_ref = refs.pop(0)
    y_refs = [refs.pop(0) for _ in range(n_y)]
    if do_ln:
        gate_ref, lng_ref, lnb_ref = refs.pop(0), refs.pop(0), refs.pop(0)
    if out_h is not None:
        sc_ref, sh_ref = refs.pop(0), refs.pop(0)
    if do_route:
        whi_ref, wlo_ref, rb_ref = refs.pop(0), refs.pop(0), refs.pop(0)
    xo_ref = refs.pop(0) if do_ln else None
    h_ref = refs.pop(0) if out_h is not None else None
    if do_route:
        ids_ref, wts_ref = refs.pop(0), refs.pop(0)
    assert not refs

    x = x_ref[...]
    if do_ln:
        y = functools.reduce(jnp.add, [_unpack_pairs(r[...]) if packed_y else r[...] for r in y_refs])
        z = alpha * x + gate_ref[0] * y
        mu = jnp.mean(z, axis=-1, keepdims=True)
        zc = z - mu
        var = jnp.mean(zc * zc, axis=-1, keepdims=True)
        x = zc * lax.rsqrt(var + LN_EPS) * lng_ref[...] + lnb_ref[...]
        xo_ref[...] = x
    if out_h is not None:
        h = x * (1.0 + sc_ref[0]) + sh_ref[0]
        h_ref[...] = _pack_pairs(h) if packed_h else h.astype(out_h)
    if do_route:
        h_hi = h.astype(BF16)
        h_lo = (h - h_hi.astype(F32)).astype(BF16)
        w_hi = whi_ref[...]
        lg = (jnp.dot(h_hi, w_hi, preferred_element_type=F32)
              + jnp.dot(h_lo, w_hi, preferred_element_type=F32)
              + jnp.dot(h_hi, wlo_ref[...], preferred_element_type=F32)) + rb_ref[...]
        ids, wts = _route_rows(lg.T)
        pad = SUBLANES - ids.shape[0]
        ids_ref[...] = jnp.concatenate([ids, jnp.zeros((pad, ids.shape[1]), jnp.int32)], axis=0)
        wts_ref[...] = jnp.concatenate([wts, jnp.zeros((pad, wts.shape[1]), F32)], axis=0)


def _ln_mod(x, ys, gate, ln_g, ln_b, sc, sh, route_w, *, lc, alpha, out_h, row_offset=0,
            packed_y=False, packed_h=False):
    t_all, d = x.shape
    do_ln = len(ys) > 0
    do_route = route_w is not None
    tm = _tile(math.gcd(lc, t_all - lc), 256)
    assert row_offset % tm == 0
    off = row_offset // tm
    t_out = t_all - row_offset
    nct = lc // tm

    row = lambda i: (i + off, 0)
    stream = lambda i: (jnp.where(i + off < nct, 0, 1), 0, 0)
    vec = pl.BlockSpec((1, d), lambda i: (0, 0))
    svec = pl.BlockSpec((1, 1, d), stream)

    args, in_specs = [x], [pl.BlockSpec((tm, d), row)]
    n_y = 0
    for y in ys:
        for grp in range(y.shape[0] // t_all):
            n_y += 1
            args.append(y)
            in_specs.append(pl.BlockSpec((tm, y.shape[1]), lambda i, grp=grp: (i + off + grp * (t_all // tm), 0)))
    if do_ln:
        args += [gate, ln_g.reshape(1, d), ln_b.reshape(1, d)]
        in_specs += [svec, vec, vec]
    if out_h is not None:
        args += [sc, sh]
        in_specs += [svec, svec]
    if do_route:
        args += list(route_w)
        in_specs += [pl.BlockSpec((d, LANES), lambda i: (0, 0)),
                     pl.BlockSpec((d, LANES), lambda i: (0, 0)),
                     pl.BlockSpec((1, LANES), lambda i: (0, 0))]
    out_shape, out_specs = [], []
    if do_ln:
        out_shape.append(jax.ShapeDtypeStruct((t_out, d), F32))
        out_specs.append(pl.BlockSpec((tm, d), lambda i: (i, 0)))
    if out_h is not None:
        hd, hdt = (d // 2, jnp.uint32) if packed_h else (d, out_h)
        out_shape.append(jax.ShapeDtypeStruct((t_out, hd), hdt))
        out_specs.append(pl.BlockSpec((tm, hd), lambda i: (i, 0)))
    if do_route:
        out_shape += [jax.ShapeDtypeStruct((SUBLANES, t_out), jnp.int32),
                      jax.ShapeDtypeStruct((SUBLANES, t_out), F32)]
        out_specs += [pl.BlockSpec((SUBLANES, tm), lambda i: (0, i))] * 2
    return pl.pallas_call(
        functools.partial(_ln_mod_kernel, n_y=n_y, do_ln=do_ln, out_h=out_h,
                          do_route=do_route, alpha=alpha, packed_y=packed_y, packed_h=packed_h),
        grid=(t_out // tm,),
        in_specs=in_specs,
        out_specs=out_specs,
        out_shape=out_shape,
        compiler_params=_params("parallel"),
        name="ln_mod",
    )(*args)


def _proj_kernel(*refs, norm, rope, scale, tn):
    refs = list(refs)
    a_ref, w_ref = refs.pop(0), refs.pop(0)
    g_ref = refs.pop(0) if norm else None
    if rope:
        cos_ref, sin_ref = refs.pop(0), refs.pop(0)
    o_ref = refs.pop(0)
    acc = jnp.dot(a_ref[...], w_ref[...], preferred_element_type=F32)
    if not (norm or rope):
        if scale != 1.0:
            acc = acc * scale
        o_ref[...] = acc.astype(o_ref.dtype)
        return
    if rope:
        cos, sin = cos_ref[...], sin_ref[...]
        lane = lax.broadcasted_iota(jnp.int32, cos.shape, 1)
        first_half = (lane % (HEAD_DIM // 2)) < (HEAD_DIM // 4)
    for hh in range(tn // HEAD_DIM):
        cols = slice(hh * HEAD_DIM, (hh + 1) * HEAD_DIM)
        t = acc[:, cols]
        if norm:
            t = t * lax.rsqrt(jnp.mean(t * t, axis=-1, keepdims=True) + RMS_EPS) * g_ref[...]
        if rope:
            up = pltpu.roll(t, HEAD_DIM - HEAD_DIM // 4, 1)
            dn = pltpu.roll(t, HEAD_DIM // 4, 1)
            t = t * cos + jnp.where(first_half, up, dn) * sin
        if scale != 1.0:
            t = t * scale
        o_ref[:, cols] = t.astype(o_ref.dtype)


def _proj(a, w, layer, *, out_dtype, norm_g=None, rope=None, scale=1.0, tm_target=768, tn_target=1024):
    t_all, kdim = a.shape
    n = w.shape[2]
    tm = _tile(t_all, tm_target)
    tn = _tile(n, tn_target, LANES)
    args = [a, w]
    in_specs = [pl.BlockSpec((tm, kdim), lambda i, j: (i, 0)),
                pl.BlockSpec((None, kdim, tn), lambda i, j: (layer, 0, j))]
    if norm_g is not None:
        args.append(norm_g.reshape(1, HEAD_DIM))
        in_specs.append(pl.BlockSpec((1, HEAD_DIM), lambda i, j: (0, 0)))
    if rope is not None:
        args += list(rope)
        in_specs += [pl.BlockSpec((tm, HEAD_DIM), lambda i, j: (i, 0))] * 2
    return pl.pallas_call(
        functools.partial(_proj_kernel, norm=norm_g is not None, rope=rope is not None,
                          scale=scale, tn=tn),
        grid=(t_all // tm, n // tn),
        in_specs=in_specs,
        out_specs=pl.BlockSpec((tm, tn), lambda i, j: (i, j)),
        out_shape=jax.ShapeDtypeStruct((t_all, n), out_dtype),
        compiler_params=_params("parallel", "parallel"),
        name="proj",
    )(*args)


def _rope_tables(n, lc):
    n_freq = HEAD_DIM // 4
    pos = jnp.arange(n, dtype=jnp.int32)
    row = (pos // GRID_W).astype(F32)
    col = (pos % GRID_W).astype(F32)
    inv_freq = ROPE_THETA ** (-jnp.arange(n_freq, dtype=F32) / n_freq)
    ar, ac = row[:, None] * inv_freq, col[:, None] * inv_freq
    cos = jnp.concatenate([jnp.cos(ar), jnp.cos(ar), jnp.cos(ac), jnp.cos(ac)], axis=1)
    sin = jnp.concatenate([-jnp.sin(ar), jnp.sin(ar), -jnp.sin(ac), jnp.sin(ac)], axis=1)
    cos = jnp.concatenate([jnp.ones((lc, HEAD_DIM), F32), cos], axis=0)
    sin = jnp.concatenate([jnp.zeros((lc, HEAD_DIM), F32), sin], axis=0)
    return cos, sin


def _sweep(qs, kcols, k_ref, v_ref, latent_rows, finish, *, lc, tk, n_lat, dv, sum_on_mxu):
    nq = len(qs)
    n = n_lat // tk

    def e0(size):
        return (lax.broadcasted_iota(jnp.int32, (size, LANES), 1) == 0).astype(v_ref.dtype)

    ones_c, ones_t = (e0(lc), e0(tk)) if sum_on_mxu else (None, None)

    def scores(lo, size):
        return [lax.dot_general(q, k_ref[pl.ds(lo, size), kc], (((1,), (1,)), ((), ())),
                                preferred_element_type=F32) for q, kc in zip(qs, kcols)]

    def pv(p, lo, size, ones):
        vc = v_ref[pl.ds(lo, size), :]
        if sum_on_mxu:
            vc = jnp.concatenate([vc, ones], axis=1)
        return jnp.dot(p, vc, preferred_element_type=F32)

    def soft(s, m, l):
        m_new = jnp.maximum(m, jnp.max(s, axis=-1, keepdims=True))
        a = jnp.exp2(m - m_new)
        p = jnp.exp2(s - m_new)
        if not sum_on_mxu:
            l = a * l + jnp.sum(p, axis=-1, keepdims=True)
        return m_new, a, p.astype(v_ref.dtype), l

    def result(acc, l):
        return acc[:, :dv] / acc[:, dv:dv + 1] if sum_on_mxu else acc / l

    ctx = []
    for s in scores(0, lc):
        m = jnp.max(s, axis=-1, keepdims=True)
        p = jnp.exp2(s - m)
        l = None if sum_on_mxu else jnp.sum(p, axis=-1, keepdims=True)
        ctx.append((m, l, pv(p.astype(v_ref.dtype), 0, lc, ones_c)))

    @pl.when(jnp.logical_not(latent_rows))
    def _():
        finish([result(acc, l) for _, l, acc in ctx])

    @pl.when(latent_rows)
    def _():
        def start(c, s):
            m0, l0, acc0 = ctx[c]
            m, a, p, l = soft(s, m0, l0)
            return (m, p, a * acc0) if sum_on_mxu else (m, l, p, a * acc0)

        state = tuple(start(c, s) for c, s in enumerate(scores(lc, tk)))

        def body(j, state):
            lo = pl.multiple_of(lc + j * tk, math.gcd(lc, tk))
            ss = scores(lo, tk)
            new = []
            for c in range(nq):
                m0, l0 = state[c][0], None if sum_on_mxu else state[c][1]
                acc = state[c][-1] + pv(state[c][-2], lo - tk, tk, ones_t)
                m, a, p, l = soft(ss[c], m0, l0)
                new.append((m, p, a * acc) if sum_on_mxu else (m, l, p, a * acc))
            return tuple(new)

        state = lax.fori_loop(1, n, body, state, unroll=True)
        outs = []
        for c in range(nq):
            l = None if sum_on_mxu else state[c][1]
            p, r = state[c][-2], state[c][-1]
            outs.append(result(r + pv(p, lc + (n - 1) * tk, tk, ones_t), l))
        finish(outs)


def _gqa_kernel(q_ref, k_ref, v_ref, o_ref, *, lc, tq, tk, n_lat):
    i = pl.program_id(1)
    heads = [slice(h * HEAD_DIM, (h + 1) * HEAD_DIM) for h in range(GQA_GROUP)]

    def finish(outs):
        for cols, o in zip(heads, outs):
            o_ref[:, cols] = o.astype(o_ref.dtype)

    _sweep([q_ref[:, cols] for cols in heads], [heads[0]] * GQA_GROUP, k_ref, v_ref, i * tq >= lc, finish,
           lc=lc, tk=tk, n_lat=n_lat, dv=HEAD_DIM, sum_on_mxu=True)


def _gqa_attention(q, k, v, *, lc, tq_target=256, tk_target=2048):
    t_all, dq = q.shape
    n_kv = k.shape[1] // HEAD_DIM
    n_lat = t_all - lc
    tq = _tile(math.gcd(lc, n_lat), tq_target)
    tk = _tile(n_lat, tk_target)
    gw = GQA_GROUP * HEAD_DIM
    return pl.pallas_call(
        functools.partial(_gqa_kernel, lc=lc, tq=tq, tk=tk, n_lat=n_lat),
        grid=(n_kv, t_all // tq),
        in_specs=[pl.BlockSpec((tq, gw), lambda g, i: (i, g)),
                  pl.BlockSpec((t_all, HEAD_DIM), lambda g, i: (0, g)),
                  pl.BlockSpec((t_all, HEAD_DIM), lambda g, i: (0, g))],
        out_specs=pl.BlockSpec((tq, gw), lambda g, i: (i, g)),
        out_shape=jax.ShapeDtypeStruct((t_all, dq), BF16),
        compiler_params=_params("parallel", "parallel"),
        name="gqa_attention",
    )(q, k, v)


def _diff_kernel(q_ref, k_ref, v_ref, lq1_ref, lk1_ref, lq2_ref, lk2_ref, g_ref, o_ref,
                 *, lc, tq, tk, n_lat, lam_init):
    i = pl.program_id(1)
    dv = 2 * HEAD_DIM

    def finish(outs):
        lam = (jnp.exp(jnp.sum(lq1_ref[...] * lk1_ref[...], axis=-1, keepdims=True))
               - jnp.exp(jnp.sum(lq2_ref[...] * lk2_ref[...], axis=-1, keepdims=True)) + lam_init)
        o = outs[0] - lam * outs[1]
        o = o * lax.rsqrt(jnp.mean(o * o, axis=-1, keepdims=True) + RMS_EPS) * g_ref[...]
        o_ref[...] = (o * (1.0 - lam_init)).astype(o_ref.dtype)

    halves = [slice(0, HEAD_DIM), slice(HEAD_DIM, dv)]
    _sweep([q_ref[:, cols] for cols in halves], halves, k_ref, v_ref, i * tq >= lc, finish,
           lc=lc, tk=tk, n_lat=n_lat, dv=dv, sum_on_mxu=False)


def _diff_attention(q, k, v, lq1, lk1, lq2, lk2, subln_g, *, lc, lam_init, tq_target=256, tk_target=1024):
    t_all, dq = q.shape
    dv = 2 * HEAD_DIM
    n_heads = dq // dv
    n_lat = t_all - lc
    tq = _tile(math.gcd(lc, n_lat), tq_target)
    tk = _tile(n_lat, tk_target)
    vec = lambda n: pl.BlockSpec((1, n), lambda h, i: (0, 0))
    return pl.pallas_call(
        functools.partial(_diff_kernel, lc=lc, tq=tq, tk=tk, n_lat=n_lat, lam_init=lam_init),
        grid=(n_heads, t_all // tq),
        in_specs=[pl.BlockSpec((tq, dv), lambda h, i: (i, h)),
                  pl.BlockSpec((t_all, dv), lambda h, i: (0, h)),
                  pl.BlockSpec((t_all, dv), lambda h, i: (0, h)),
                  vec(HEAD_DIM), vec(HEAD_DIM), vec(HEAD_DIM), vec(HEAD_DIM), vec(dv)],
        out_specs=pl.BlockSpec((tq, dv), lambda h, i: (i, h)),
        out_shape=jax.ShapeDtypeStruct((t_all, v.shape[1]), BF16),
        compiler_params=_params("parallel", "parallel"),
        name="diff_attention",
    )(q, k, v, lq1.reshape(1, -1), lk1.reshape(1, -1), lq2.reshape(1, -1), lk2.reshape(1, -1),
      subln_g.reshape(1, -1))


def _pool_kernel(x_ref, xp_ref, xn_ref, sc_ref, sh_ref, w_ref, b_ref, ps_ref, o_ref, buf_ref,
                 *, lc, n_lat, tm):
    i = pl.program_id(0)
    nct = lc // tm
    nt = (lc + n_lat) // tm
    sc, sh = sc_ref[0], sh_ref[0]
    first = jnp.logical_or(i == 0, i == nct)
    last = jnp.logical_or(i == nct - 1, i == nt - 1)
    cur = x_ref[...] * (1.0 + sc) + sh
    buf_ref[0:POOL_HALO, :] = jnp.where(first, 0.0, xp_ref[...] * (1.0 + sc) + sh)
    buf_ref[POOL_HALO:POOL_HALO + tm, :] = cur
    buf_ref[POOL_HALO + tm:2 * POOL_HALO + tm, :] = jnp.where(last, 0.0, xn_ref[...] * (1.0 + sc) + sh)

    in_ctx = i < nct
    t_loc = (lax.broadcasted_iota(jnp.int32, (tm, 1), 0)
             + (i - jnp.where(in_ctx, 0, nct)) * tm)
    n_s = jnp.where(in_ctx, lc, n_lat)
    cdim = w_ref.shape[1]
    for g, win in enumerate(POOL_WINDOWS):
        cols = slice(g * cdim, (g + 1) * cdim)
        half = win // 2
        acc = buf_ref[POOL_HALO - half:POOL_HALO - half + tm, cols]
        for k in range(-half + 1, half):
            acc = acc + buf_ref[POOL_HALO + k:POOL_HALO + k + tm, cols]
        cnt = jnp.minimum(t_loc + half - 1, n_s - 1) - jnp.maximum(t_loc - half, 0) + 1
        y = acc / cnt.astype(F32) - cur[:, cols]
        out = jnp.dot(y.astype(BF16), w_ref[g], preferred_element_type=F32) + b_ref[g]
        o_ref[:, cols] = out * ps_ref[:, cols]


def _pool_mixer(x, sc, sh, pool_w, pool_b, pool_scale, *, lc):
    t_all, d = x.shape
    n_lat = t_all - lc
    ng, cdim, _ = pool_w.shape
    tm = _tile(math.gcd(lc, n_lat), 128)
    hb = tm // POOL_HALO
    n_hb = t_all // POOL_HALO
    stream = lambda i: (jnp.where(i < lc // tm, 0, 1), 0, 0)
    return pl.pallas_call(
        functools.partial(_pool_kernel, lc=lc, n_lat=n_lat, tm=tm),
        grid=(t_all // tm,),
        in_specs=[pl.BlockSpec((tm, d), lambda i: (i, 0)),
                  pl.BlockSpec((POOL_HALO, d), lambda i: (jnp.maximum(i * hb - 1, 0), 0)),
                  pl.BlockSpec((POOL_HALO, d), lambda i: (jnp.minimum((i + 1) * hb, n_hb - 1), 0)),
                  pl.BlockSpec((1, 1, d), stream),
                  pl.BlockSpec((1, 1, d), stream),
                  pl.BlockSpec((ng, cdim, cdim), lambda i: (0, 0, 0)),
                  pl.BlockSpec((ng, 1, cdim), lambda i: (0, 0, 0)),
                  pl.BlockSpec((1, d), lambda i: (0, 0))],
        out_specs=pl.BlockSpec((tm, d), lambda i: (i, 0)),
        out_shape=jax.ShapeDtypeStruct((t_all, d), F32),
        scratch_shapes=[pltpu.VMEM((tm + 2 * POOL_HALO, d), F32)],
        compiler_params=_params("parallel"),
        name="pool_mixer",
    )(x, x, x, sc, sh, pool_w.astype(BF16), pool_b.reshape(ng, 1, cdim), pool_scale.reshape(1, d))


def _moe_kernel(texp_ref, rtok_ref, rdst_ref, nact_ref,
                h_hbm, gate_ref, w1_ref, w3_ref, w2_ref, y_hbm,
                xbuf, ybuf, sem_in, sem_out, *, tm, n_tiles):
    i = pl.program_id(0)
    n = nact_ref[0]
    slot = lax.rem(i, 2)
    spare = y_hbm.shape[0] - 2 * tm

    def gather(tile, s, k):
        tok = rtok_ref[tile * tm + k]
        return pltpu.make_async_copy(h_hbm.at[pl.ds(tok, 1)], xbuf.at[s, pl.ds(k, 1)], sem_in.at[s])

    def scatter(tile, s, k):
        dst = rdst_ref[(tile + 1) * tm + k]
        return pltpu.make_async_copy(ybuf.at[s, pl.ds(k, 1)], y_hbm.at[pl.ds(dst, 1)], sem_out.at[s])

    def fill(k):
        return pltpu.make_async_copy(ybuf.at[0, pl.ds(k, 1)], y_hbm.at[pl.ds(spare + k, 1)], sem_out.at[0])

    @pl.when(i == 0)
    def _():
        ybuf[...] = jnp.zeros_like(ybuf)
        for k in range(tm):
            fill(k).start()
        for k in range(tm):
            gather(0, 0, k).start()
        for k in range(tm):
            fill(k).wait()

    @pl.when(jnp.logical_and(i >= 1, i < n))
    def _():
        for k in range(tm):
            scatter(i - 2, slot, k).wait()

    @pl.when(i < n)
    def _():
        for k in range(tm):
            gather(i, slot, k).wait()
        nxt = jnp.minimum(i + 1, n_tiles - 1)
        x = _unpack_pairs(xbuf[slot]).astype(BF16)
        for k in range(tm):
            gather(nxt, 1 - slot, k).start()
        a = jnp.dot(x, w1_ref[0], preferred_element_type=F32)
        u = jnp.dot(x, w3_ref[0], preferred_element_type=F32)
        for k in range(tm):
            scatter(i - 1, 1 - slot, k).start(priority=1)
        hid = a * _sigmoid(a) * u * gate_ref[...]
        ybuf[slot] = _pack_pairs(jnp.dot(hid.astype(BF16), w2_ref[0], preferred_element_type=F32))

        @pl.when(i == n - 1)
        def _():
            for k in range(tm):
                scatter(i, slot, k).start()
            for k in range(tm):
                gather(nxt, 1 - slot, k).wait()
            for k in range(tm):
                scatter(i - 1, 1 - slot, k).wait()
            for k in range(tm):
                scatter(i, slot, k).wait()


def _moe(h, ids, wts, w1, w3, w2, layer, *, tm_target=256):
    _, n_exp, d, f = w1.shape
    t_all = h.shape[0]
    tm = _tile(t_all, tm_target)
    n_pairs = 2 * t_all
    n_tiles = n_pairs // tm + n_exp
    n_rows = n_tiles * tm

    e_flat = ids.reshape(n_pairs)
    w_flat = wts.reshape(n_pairs)
    onehot = (e_flat[:, None] == jnp.arange(n_exp, dtype=jnp.int32)[None, :]).astype(jnp.int32)
    rank = jnp.sum((jnp.cumsum(onehot, axis=0) - onehot) * onehot, axis=1)
    counts = jnp.sum(onehot, axis=0)
    padded = (counts + tm - 1) // tm * tm
    p_end = jnp.cumsum(padded)
    pos = jnp.sum(onehot * (p_end - padded)[None, :], axis=1) + rank
    row_pair = jnp.full((n_rows,), -1, jnp.int32).at[pos].set(jnp.arange(n_pairs, dtype=jnp.int32))
    valid = row_pair >= 0
    safe = jnp.maximum(row_pair, 0)
    row_tok = jnp.where(valid, safe % t_all, 0)
    row_gate = jnp.where(valid, w_flat[safe], 0.0).reshape(n_rows, 1)
    rows = jnp.arange(-tm, n_rows, dtype=jnp.int32)
    spare_row = n_pairs + (rows // tm) % 2 * tm + rows % tm
    row_dst = jnp.where(rows >= 0, jnp.where(jnp.concatenate([jnp.zeros((tm,), bool), valid]),
                                             jnp.concatenate([jnp.zeros((tm,), jnp.int32), row_pair]),
                                             spare_row), spare_row)
    n_act = (p_end[-1] // tm).astype(jnp.int32)
    tile_start = jnp.arange(n_tiles, dtype=jnp.int32) * tm
    tile_exp = jnp.sum(p_end[None, :] <= tile_start[:, None], axis=1, dtype=jnp.int32)
    last_exp = jnp.sum(p_end <= (n_act - 1) * tm, dtype=jnp.int32)
    tile_exp = jnp.minimum(jnp.where(tile_start < n_act * tm, tile_exp, last_exp), n_exp - 1)

    wspec = lambda shape: pl.BlockSpec((None,) + shape, lambda i, te, rt, rd, na: (layer, te[i], 0, 0))
    return pl.pallas_call(
        functools.partial(_moe_kernel, tm=tm, n_tiles=n_tiles),
        grid_spec=pltpu.PrefetchScalarGridSpec(
            num_scalar_prefetch=4,
            grid=(n_tiles,),
            in_specs=[pl.BlockSpec(memory_space=pl.ANY),
                      pl.BlockSpec((tm, 1), lambda i, te, rt, rd, na: (i, 0)),
                      wspec((1, d, f)), wspec((1, d, f)), wspec((1, f, d))],
            out_specs=pl.BlockSpec(memory_space=pl.ANY),
            scratch_shapes=[pltpu.VMEM((2, tm, d // 2), jnp.uint32)] * 2 + [
                            pltpu.SemaphoreType.DMA((2,)), pltpu.SemaphoreType.DMA((2,))],
        ),
        out_shape=jax.ShapeDtypeStruct((n_pairs + 2 * tm, d // 2), jnp.uint32),
        compiler_params=_params("arbitrary"),
        name="moe_experts",
    )(tile_exp, row_tok, row_dst, n_act.reshape(1), h, row_gate,
      w1, w3, w2)


def _router_weights(rg_w, rg_b, re_w, re_b):
    d = rg_w.shape[0]
    n = rg_w.shape[1] + re_w.shape[1]
    w = jnp.concatenate([rg_w, re_w, jnp.zeros((d, LANES - n), F32)], axis=1)
    b = jnp.concatenate([rg_b, re_b, jnp.zeros((LANES - n,), F32)]).reshape(1, LANES)
    w_hi = w.astype(BF16)
    w_lo = (w - w_hi.astype(F32)).astype(BF16)
    return w_hi, w_lo, b


def kernel(x, c, ctx, c_ctx, ada_w, ada_b, ln_g, ln_b, attn_wq, attn_wk, attn_wv, attn_wo, attn_qn_g, attn_kn_g, pool_w, pool_b, pool_scale, diff_wq, diff_wk, diff_wv, diff_wo, diff_lq1, diff_lk1, diff_lq2, diff_lk2, diff_subln_g, moe_rg_w, moe_rg_b, moe_re_w, moe_re_b, moe_w1, moe_w3, moe_w2):
    b, n, d = x.shape
    assert b == 1 and c.shape[0] == 1 and ctx.shape[0] == 1
    lc = ctx.shape[1]
    depth = ada_w.shape[0]
    alpha = (2 * depth) ** 0.25
    qk_scale = math.log2(math.e) / math.sqrt(HEAD_DIM)

    s = jnp.concatenate([ctx[0], x[0]], axis=0)
    mods = _ada_all(jnp.concatenate([c_ctx[None, :], c], axis=0), ada_w, ada_b)
    mod = lambda i, k: mods[i, :, k * d:(k + 1) * d].reshape(2, 1, d)
    rope = _rope_tables(n, lc)
    bf = lambda w: w.astype(BF16)
    attn_w = [bf(w) for w in (attn_wq, attn_wk, attn_wv, attn_wo)]
    diff_w = [bf(w) for w in (diff_wq, diff_wk, diff_wv, diff_wo)]
    moe_w = [bf(w) for w in (moe_w1, moe_w3, moe_w2)]

    h1 = None
    for i in range(depth):
        last = i == depth - 1
        kind, j = i % N_MIXERS, i // N_MIXERS
        if kind != 1 and h1 is None:
            (h1,) = _ln_mod(s, (), None, None, None, mod(i, 1), mod(i, 0), None,
                            lc=lc, alpha=alpha, out_h=BF16)
        if kind == 0:
            q = _proj(h1, attn_w[0], j, out_dtype=BF16, norm_g=attn_qn_g[j], rope=rope, scale=qk_scale)
            k = _proj(h1, attn_w[1], j, out_dtype=BF16, norm_g=attn_kn_g[j], rope=rope)
            v = _proj(h1, attn_w[2], j, out_dtype=BF16)
            o = _gqa_attention(q, k, v, lc=lc)
            y = _proj(o, attn_w[3], j, out_dtype=F32)
        elif kind == 1:
            y = _pool_mixer(s, mod(i, 1), mod(i, 0), pool_w[j], pool_b[j], pool_scale[j], lc=lc)
        else:
            lam_init = 0.8 - 0.6 * math.exp(-0.3 * i)
            q = _proj(h1, diff_w[0], j, out_dtype=BF16, rope=rope, scale=qk_scale)
            k = _proj(h1, diff_w[1], j, out_dtype=BF16, rope=rope)
            v = _proj(h1, diff_w[2], j, out_dtype=BF16)
            o = _diff_attention(q, k, v, diff_lq1[j], diff_lk1[j], diff_lq2[j], diff_lk2[j],
                                diff_subln_g[j], lc=lc, lam_init=lam_init)
            y = _proj(o, diff_w[3], j, out_dtype=F32)
        h1 = None
        route_w = _router_weights(moe_rg_w[i], moe_rg_b[i], moe_re_w[i], moe_re_b[i])
        s, h2, ids, wts = _ln_mod(s, (y,), mod(i, 2), ln_g[i, 0], ln_b[i, 0], mod(i, 4), mod(i, 3),
                                  route_w, lc=lc, alpha=alpha, out_h=F32, packed_h=True)
        y2 = _moe(h2, ids[:2], wts[:2], *moe_w, i)
        if last:
            (s,) = _ln_mod(s, (y2,), mod(i, 5), ln_g[i, 1], ln_b[i, 1], None, None, None,
                           lc=lc, alpha=alpha, out_h=None, row_offset=lc, packed_y=True)
        elif (i + 1) % N_MIXERS == 1:
            (s,) = _ln_mod(s, (y2,), mod(i, 5), ln_g[i, 1], ln_b[i, 1], None, None, None,
                           lc=lc, alpha=alpha, out_h=None, packed_y=True)
        else:
            s, h1 = _ln_mod(s, (y2,), mod(i, 5), ln_g[i, 1], ln_b[i, 1], mod(i + 1, 1), mod(i + 1, 0),
                            None, lc=lc, alpha=alpha, out_h=BF16, packed_y=True)
    return s[None]
```

```python
import functools
import math

import jax
import jax.numpy as jnp
from jax import lax
from jax.experimental import pallas as pl
from jax.experimental.pallas import tpu as pltpu

HEAD_DIM = 128
GRID_W = 64
ROPE_THETA = 10000.0
POOL_WINDOWS = (2, 4, 8, 16)
POOL_HALO = 8
N_EXPERT_GROUPS = 4
EXPERTS_PER_GROUP = 4
N_EXPERTS = N_EXPERT_GROUPS * EXPERTS_PER_GROUP
N_MIXERS = 3
GQA_GROUP = 4
LN_EPS = 1e-6
RMS_EPS = 1e-6
LANES = 128
SUBLANES = 8
VMEM_LIMIT_BYTES = 56 * 1024 * 1024

F32 = jnp.float32
BF16 = jnp.bfloat16


def _params(*sem):
    return pltpu.CompilerParams(dimension_semantics=sem, vmem_limit_bytes=VMEM_LIMIT_BYTES)


def _tile(n, target, mult=SUBLANES):
    best = None
    for t in range(mult, min(n, target) + 1, mult):
        if n % t == 0:
            best = t
    assert best is not None, (n, target, mult)
    return best


def _sigmoid(v):
    return 1.0 / (1.0 + jnp.exp(-v))


def _pack_pairs(v):
    half = v.shape[1] // 2
    lo = lax.bitcast_convert_type(v[:, :half].astype(BF16).astype(F32), jnp.uint32)
    hi = lax.bitcast_convert_type(v[:, half:].astype(BF16).astype(F32), jnp.uint32)
    return hi | (lo >> 16)


def _unpack_pairs(u):
    lo = lax.bitcast_convert_type(u << 16, F32)
    hi = lax.bitcast_convert_type(u & jnp.uint32(0xFFFF0000), F32)
    return jnp.concatenate([lo, hi], axis=1)


def _ada_kernel(c_ref, w_ref, b_ref, o_ref, acc_ref, *, tk, tn):
    k = pl.program_id(2)

    @pl.when(k == 0)
    def _():
        acc_ref[...] = jnp.zeros_like(acc_ref)

    cv = c_ref[...]
    sv = cv * _sigmoid(cv)
    s0, s1 = sv[0], sv[1]
    for j in range(tn // LANES):
        cols = slice(j * LANES, (j + 1) * LANES)
        wj = w_ref[0, :, cols]
        acc_ref[0, :, cols] += (wj * s0).reshape(tk // SUBLANES, SUBLANES, LANES).sum(axis=0)
        acc_ref[1, :, cols] += (wj * s1).reshape(tk // SUBLANES, SUBLANES, LANES).sum(axis=0)

    @pl.when(k == pl.num_programs(2) - 1)
    def _():
        o_ref[0] = acc_ref[...].sum(axis=1) + b_ref[0]


def _ada_all(cvec, ada_w, ada_b):
    n_layers, d, n_out = ada_w.shape
    tk = _tile(d, 1024)
    tn = _tile(n_out, 2048, LANES)
    c_rep = jnp.broadcast_to(cvec[:, :, None], (2, d, LANES))
    return pl.pallas_call(
        functools.partial(_ada_kernel, tk=tk, tn=tn),
        grid=(n_layers, n_out // tn, d // tk),
        in_specs=[
            pl.BlockSpec((2, tk, LANES), lambda l, j, k: (0, k, 0)),
            pl.BlockSpec((1, tk, tn), lambda l, j, k: (l, k, j)),
            pl.BlockSpec((1, 1, tn), lambda l, j, k: (l, 0, j)),
        ],
        out_specs=pl.BlockSpec((1, 2, tn), lambda l, j, k: (l, 0, j)),
        out_shape=jax.ShapeDtypeStruct((n_layers, 2, n_out), F32),
        scratch_shapes=[pltpu.VMEM((2, SUBLANES, tn), F32)],
        compiler_params=_params("parallel", "parallel", "arbitrary"),
        name="ada_mod",
    )(c_rep, ada_w, ada_b.reshape(n_layers, 1, n_out))


def _route_rows(lg):
    ng, ne = N_EXPERT_GROUPS, EXPERTS_PER_GROUP
    g = [lg[r:r + 1, :] for r in range(ng)]
    gmax = functools.reduce(jnp.maximum, g)
    gidx = jnp.full(gmax.shape, ng - 1, jnp.int32)
    for r in range(ng - 2, -1, -1):
        gidx = jnp.where(g[r] == gmax, r, gidx)
    p_g = 1.0 / functools.reduce(jnp.add, [jnp.exp(gr - gmax) for gr in g])
    sel = []
    for j in range(ne):
        v = lg[ng + (ng - 1) * ne + j:ng + (ng - 1) * ne + j + 1, :]
        for r in range(ng - 2, -1, -1):
            v = jnp.where(gidx == r, lg[ng + r * ne + j:ng + r * ne + j + 1, :], v)
        sel.append(v)
    v1 = functools.reduce(jnp.maximum, sel)
    i1 = jnp.full(v1.shape, ne - 1, jnp.int32)
    for j in range(ne - 2, -1, -1):
        i1 = jnp.where(sel[j] == v1, j, i1)
    rest = [jnp.where(i1 == j, -jnp.inf, sel[j]) for j in range(ne)]
    v2 = functools.reduce(jnp.maximum, rest)
    i2 = jnp.full(v2.shape, ne - 1, jnp.int32)
    for j in range(ne - 2, -1, -1):
        i2 = jnp.where(rest[j] == v2, j, i2)
    t = jnp.exp(v2 - v1)
    w1 = p_g / (1.0 + t)
    w2 = p_g * t / (1.0 + t)
    ids = jnp.concatenate([gidx * ne + i1, gidx * ne + i2], axis=0)
    wts = jnp.concatenate([w1, w2], axis=0)
    return ids, wts


def _ln_mod_kernel(*refs, n_y, do_ln, out_h, do_route, alpha, packed_y, packed_h):
    refs = list(refs)
    x_ref = refs.pop(0)
    y_refs = [refs.pop(0) for _ in range(n_y)]
    if do_ln:
        gate_ref, lng_ref, lnb_ref = refs.pop(0), refs.pop(0), refs.pop(0)
    if out_h is not None:
        sc_ref, sh_ref = refs.pop(0), refs.pop(0)
    if do_route:
        whi_ref, wlo_ref, rb_ref = refs.pop(0), refs.pop(0), refs.pop(0)
    xo_ref = refs.pop(0) if do_ln else None
    h_ref = refs.pop(0) if out_h is not None else None
    if do_route:
        ids_ref, wts_ref = refs.pop(0), refs.pop(0)
    assert not refs

    x = x_ref[...]
    if do_ln:
        y = functools.reduce(jnp.add, [_unpack_pairs(r[...]) if packed_y else r[...] for r in y_refs])
        z = alpha * x + gate_ref[0] * y
        mu = jnp.mean(z, axis=-1, keepdims=True)
        zc = z - mu
        var = jnp.mean(zc * zc, axis=-1, keepdims=True)
        x = zc * lax.rsqrt(var + LN_EPS) * lng_ref[...] + lnb_ref[...]
        xo_ref[...] = x
    if out_h is not None:
        h = x * (1.0 + sc_ref[0]) + sh_ref[0]
        h_ref[...] = _pack_pairs(h) if packed_h else h.astype(out_h)
    if do_route:
        h_hi = h.astype(BF16)
        h_lo = (h - h_hi.astype(F32)).astype(BF16)
        w_hi = whi_ref[...]
        lg = (jnp.dot(h_hi, w_hi, preferred_element_type=F32)
              + jnp.dot(h_lo, w_hi, preferred_element_type=F32)
              + jnp.dot(h_hi, wlo_ref[...], preferred_element_type=F32)) + rb_ref[...]
        ids, wts = _route_rows(lg.T)
        pad = SUBLANES - ids.shape[0]
        ids_ref[...] = jnp.concatenate([ids, jnp.zeros((pad, ids.shape[1]), jnp.int32)], axis=0)
        wts_ref[...] = jnp.concatenate([wts, jnp.zeros((pad, wts.shape[1]), F32)], axis=0)


def _ln_mod(x, ys, gate, ln_g, ln_b, sc, sh, route_w, *, lc, alpha, out_h, row_offset=0,
            packed_y=False, packed_h=False):
    t_all, d = x.shape
    do_ln = len(ys) > 0
    do_route = route_w is not None
    tm = _tile(math.gcd(lc, t_all - lc), 256)
    assert row_offset % tm == 0
    off = row_offset // tm
    t_out = t_all - row_offset
    nct = lc // tm

    row = lambda i: (i + off, 0)
    stream = lambda i: (jnp.where(i + off < nct, 0, 1), 0, 0)
    vec = pl.BlockSpec((1, d), lambda i: (0, 0))
    svec = pl.BlockSpec((1, 1, d), stream)

    args, in_specs = [x], [pl.BlockSpec((tm, d), row)]
    n_y = 0
    for y in ys:
        for grp in range(y.shape[0] // t_all):
            n_y += 1
            args.append(y)
            in_specs.append(pl.BlockSpec((tm, y.shape[1]), lambda i, grp=grp: (i + off + grp * (t_all // tm), 0)))
    if do_ln:
        args += [gate, ln_g.reshape(1, d), ln_b.reshape(1, d)]
        in_specs += [svec, vec, vec]
    if out_h is not None:
        args += [sc, sh]
        in_specs += [svec, svec]
    if do_route:
        args += list(route_w)
        in_specs += [pl.BlockSpec((d, LANES), lambda i: (0, 0)),
                     pl.BlockSpec((d, LANES), lambda i: (0, 0)),
                     pl.BlockSpec((1, LANES), lambda i: (0, 0))]
    out_shape, out_specs = [], []
    if do_ln:
        out_shape.append(jax.ShapeDtypeStruct((t_out, d), F32))
        out_specs.append(pl.BlockSpec((tm, d), lambda i: (i, 0)))
    if out_h is not None:
        hd, hdt = (d // 2, jnp.uint32) if packed_h else (d, out_h)
        out_shape.append(jax.ShapeDtypeStruct((t_out, hd), hdt))
        out_specs.append(pl.BlockSpec((tm, hd), lambda i: (i, 0)))
    if do_route:
        out_shape += [jax.ShapeDtypeStruct((SUBLANES, t_out), jnp.int32),
                      jax.ShapeDtypeStruct((SUBLANES, t_out), F32)]
        out_specs += [pl.BlockSpec((SUBLANES, tm), lambda i: (0, i))] * 2
    return pl.pallas_call(
        functools.partial(_ln_mod_kernel, n_y=n_y, do_ln=do_ln, out_h=out_h,
                          do_route=do_route, alpha=alpha, packed_y=packed_y, packed_h=packed_h),
        grid=(t_out // tm,),
        in_specs=in_specs,
        out_specs=out_specs,
        out_shape=out_shape,
        compiler_params=_params("parallel"),
        name="ln_mod",
    )(*args)


def _proj_kernel(*refs, norm, rope, scale, tn):
    refs = list(refs)
    a_ref, w_ref = refs.pop(0), refs.pop(0)
    g_ref = refs.pop(0) if norm else None
    if rope:
        cos_ref, sin_ref = refs.pop(0), refs.pop(0)
    o_ref = refs.pop(0)
    acc = jnp.dot(a_ref[...], w_ref[...], preferred_element_type=F32)
    if not (norm or rope):
        if scale != 1.0:
            acc = acc * scale
        o_ref[...] = acc.astype(o_ref.dtype)
        return
    if rope:
        cos, sin = cos_ref[...], sin_ref[...]
        lane = lax.broadcasted_iota(jnp.int32, cos.shape, 1)
        first_half = (lane % (HEAD_DIM // 2)) < (HEAD_DIM // 4)
    for hh in range(tn // HEAD_DIM):
        cols = slice(hh * HEAD_DIM, (hh + 1) * HEAD_DIM)
        t = acc[:, cols]
        if norm:
            t = t * lax.rsqrt(jnp.mean(t * t, axis=-1, keepdims=True) + RMS_EPS) * g_ref[...]
        if rope:
            up = pltpu.roll(t, HEAD_DIM - HEAD_DIM // 4, 1)
            dn = pltpu.roll(t, HEAD_DIM // 4, 1)
            t = t * cos + jnp.where(first_half, up, dn) * sin
        if scale != 1.0:
            t = t * scale
        o_ref[:, cols] = t.astype(o_ref.dtype)


def _proj(a, w, layer, *, out_dtype, norm_g=None, rope=None, scale=1.0, tm_target=768, tn_target=1024):
    t_all, kdim = a.shape
    n = w.shape[2]
    tm = _tile(t_all, tm_target)
    tn = _tile(n, tn_target, LANES)
    args = [a, w]
    in_specs = [pl.BlockSpec((tm, kdim), lambda i, j: (i, 0)),
                pl.BlockSpec((None, kdim, tn), lambda i, j: (layer, 0, j))]
    if norm_g is not None:
        args.append(norm_g.reshape(1, HEAD_DIM))
        in_specs.append(pl.BlockSpec((1, HEAD_DIM), lambda i, j: (0, 0)))
    if rope is not None:
        args += list(rope)
        in_specs += [pl.BlockSpec((tm, HEAD_DIM), lambda i, j: (i, 0))] * 2
    return pl.pallas_call(
        functools.partial(_proj_kernel, norm=norm_g is not None, rope=rope is not None,
                          scale=scale, tn=tn),
        grid=(t_all // tm, n // tn),
        in_specs=in_specs,
        out_specs=pl.BlockSpec((tm, tn), lambda i, j: (i, j)),
        out_shape=jax.ShapeDtypeStruct((t_all, n), out_dtype),
        compiler_params=_params("parallel", "parallel"),
        name="proj",
    )(*args)


def _rope_tables(n, lc):
    n_freq = HEAD_DIM // 4
    pos = jnp.arange(n, dtype=jnp.int32)
    row = (pos // GRID_W).astype(F32)
    col = (pos % GRID_W).astype(F32)
    inv_freq = ROPE_THETA ** (-jnp.arange(n_freq, dtype=F32) / n_freq)
    ar, ac = row[:, None] * inv_freq, col[:, None] * inv_freq
    cos = jnp.concatenate([jnp.cos(ar), jnp.cos(ar), jnp.cos(ac), jnp.cos(ac)], axis=1)
    sin = jnp.concatenate([-jnp.sin(ar), jnp.sin(ar), -jnp.sin(ac), jnp.sin(ac)], axis=1)
    cos = jnp.concatenate([jnp.ones((lc, HEAD_DIM), F32), cos], axis=0)
    sin = jnp.concatenate([jnp.zeros((lc, HEAD_DIM), F32), sin], axis=0)
    return cos, sin


def _sweep(qs, kcols, k_ref, v_ref, latent_rows, finish, *, lc, tk, n_lat, dv, sum_on_mxu):
    nq = len(qs)
    n = n_lat // tk

    def e0(size):
        return (lax.broadcasted_iota(jnp.int32, (size, LANES), 1) == 0).astype(v_ref.dtype)

    ones_c, ones_t = (e0(lc), e0(tk)) if sum_on_mxu else (None, None)

    def scores(lo, size):
        return [lax.dot_general(q, k_ref[pl.ds(lo, size), kc], (((1,), (1,)), ((), ())),
                                preferred_element_type=F32) for q, kc in zip(qs, kcols)]

    def pv(p, lo, size, ones):
        vc = v_ref[pl.ds(lo, size), :]
        if sum_on_mxu:
            vc = jnp.concatenate([vc, ones], axis=1)
        return jnp.dot(p, vc, preferred_element_type=F32)

    def soft(s, m, l):
        m_new = jnp.maximum(m, jnp.max(s, axis=-1, keepdims=True))
        a = jnp.exp2(m - m_new)
        p = jnp.exp2(s - m_new)
        if not sum_on_mxu:
            l = a * l + jnp.sum(p, axis=-1, keepdims=True)
        return m_new, a, p.astype(v_ref.dtype), l

    def result(acc, l):
        return acc[:, :dv] / acc[:, dv:dv + 1] if sum_on_mxu else acc / l

    def context():
        out = []
        for s in scores(0, lc):
            m = jnp.max(s, axis=-1, keepdims=True)
            p = jnp.exp2(s - m)
            l = None if sum_on_mxu else jnp.sum(p, axis=-1, keepdims=True)
            out.append((m, l, pv(p.astype(v_ref.dtype), 0, lc, ones_c)))
        return out

    @pl.when(jnp.logical_not(latent_rows))
    def _():
        finish([result(acc, l) for _, l, acc in context()])

    @pl.when(latent_rows)
    def _():
        ctx = context()

        def start(c, s):
            m0, l0, acc0 = ctx[c]
            m, a, p, l = soft(s, m0, l0)
            return (m, p, a * acc0) if sum_on_mxu else (m, l, p, a * acc0)

        state = tuple(start(c, s) for c, s in enumerate(scores(lc, tk)))

        def body(j, state):
            lo = pl.multiple_of(lc + j * tk, math.gcd(lc, tk))
            ss = scores(lo, tk)
            new = []
            for c in range(nq):
                m0, l0 = state[c][0], None if sum_on_mxu else state[c][1]
                acc = state[c][-1] + pv(state[c][-2], lo - tk, tk, ones_t)
                m, a, p, l = soft(ss[c], m0, l0)
                new.append((m, p, a * acc) if sum_on_mxu else (m, l, p, a * acc))
            return tuple(new)

        state = lax.fori_loop(1, n, body, state, unroll=True)
        outs = []
        for c in range(nq):
            l = None if sum_on_mxu else state[c][1]
            p, r = state[c][-2], state[c][-1]
            outs.append(result(r + pv(p, lc + (n - 1) * tk, tk, ones_t), l))
        finish(outs)


def _gqa_kernel(q_ref, k_ref, v_ref, o_ref, *, lc, tq, tk, n_lat):
    i = pl.program_id(1)
    heads = [slice(h * HEAD_DIM, (h + 1) * HEAD_DIM) for h in range(GQA_GROUP)]

    def finish(outs):
        for cols, o in zip(heads, outs):
            o_ref[:, cols] = o.astype(o_ref.dtype)

    _sweep([q_ref[:, cols] for cols in heads], [heads[0]] * GQA_GROUP, k_ref, v_ref, i * tq >= lc, finish,
           lc=lc, tk=tk, n_lat=n_lat, dv=HEAD_DIM, sum_on_mxu=True)


def _gqa_attention(q, k, v, *, lc, tq_target=256, tk_target=2048):
    t_all, dq = q.shape
    n_kv = k.shape[1] // HEAD_DIM
    n_lat = t_all - lc
    tq = _tile(math.gcd(lc, n_lat), tq_target)
    tk = _tile(n_lat, tk_target)
    gw = GQA_GROUP * HEAD_DIM
    return pl.pallas_call(
        functools.partial(_gqa_kernel, lc=lc, tq=tq, tk=tk, n_lat=n_lat),
        grid=(n_kv, t_all // tq),
        in_specs=[pl.BlockSpec((tq, gw), lambda g, i: (i, g)),
                  pl.BlockSpec((t_all, HEAD_DIM), lambda g, i: (0, g)),
                  pl.BlockSpec((t_all, HEAD_DIM), lambda g, i: (0, g))],
        out_specs=pl.BlockSpec((tq, gw), lambda g, i: (i, g)),
        out_shape=jax.ShapeDtypeStruct((t_all, dq), BF16),
        compiler_params=_params("parallel", "parallel"),
        name="gqa_attention",
    )(q, k, v)


def _diff_kernel(q_ref, k_ref, v_ref, lq1_ref, lk1_ref, lq2_ref, lk2_ref, g_ref, o_ref,
                 *, lc, tq, tk, n_lat, lam_init):
    i = pl.program_id(1)
    dv = 2 * HEAD_DIM

    def finish(outs):
        lam = (jnp.exp(jnp.sum(lq1_ref[...] * lk1_ref[...], axis=-1, keepdims=True))
               - jnp.exp(jnp.sum(lq2_ref[...] * lk2_ref[...], axis=-1, keepdims=True)) + lam_init)
        o = outs[0] - lam * outs[1]
        o = o * lax.rsqrt(jnp.mean(o * o, axis=-1, keepdims=True) + RMS_EPS) * g_ref[...]
        o_ref[...] = (o * (1.0 - lam_init)).astype(o_ref.dtype)

    halves = [slice(0, HEAD_DIM), slice(HEAD_DIM, dv)]
    _sweep([q_ref[:, cols] for cols in halves], halves, k_ref, v_ref, i * tq >= lc, finish,
           lc=lc, tk=tk, n_lat=n_lat, dv=dv, sum_on_mxu=False)


def _diff_attention(q, k, v, lq1, lk1, lq2, lk2, subln_g, *, lc, lam_init, tq_target=256, tk_target=1024):
    t_all, dq = q.shape
    dv = 2 * HEAD_DIM
    n_heads = dq // dv
    n_lat = t_all - lc
    tq = _tile(math.gcd(lc, n_lat), tq_target)
    tk = _tile(n_lat, tk_target)
    vec = lambda n: pl.BlockSpec((1, n), lambda h, i: (0, 0))
    return pl.pallas_call(
        functools.partial(_diff_kernel, lc=lc, tq=tq, tk=tk, n_lat=n_lat, lam_init=lam_init),
        grid=(n_heads, t_all // tq),
        in_specs=[pl.BlockSpec((tq, dv), lambda h, i: (i, h)),
                  pl.BlockSpec((t_all, dv), lambda h, i: (0, h)),
                  pl.BlockSpec((t_all, dv), lambda h, i: (0, h)),
                  vec(HEAD_DIM), vec(HEAD_DIM), vec(HEAD_DIM), vec(HEAD_DIM), vec(dv)],
        out_specs=pl.BlockSpec((tq, dv), lambda h, i: (i, h)),
        out_shape=jax.ShapeDtypeStruct((t_all, v.shape[1]), BF16),
        compiler_params=_params("parallel", "parallel"),
        name="diff_attention",
    )(q, k, v, lq1.reshape(1, -1), lk1.reshape(1, -1), lq2.reshape(1, -1), lk2.reshape(1, -1),
      subln_g.reshape(1, -1))


def _pool_kernel(x_ref, xp_ref, xn_ref, sc_ref, sh_ref, w_ref, b_ref, ps_ref, o_ref, buf_ref,
                 *, lc, n_lat, tm):
    i = pl.program_id(0)
    nct = lc // tm
    nt = (lc + n_lat) // tm
    sc, sh = sc_ref[0], sh_ref[0]
    first = jnp.logical_or(i == 0, i == nct)
    last = jnp.logical_or(i == nct - 1, i == nt - 1)
    cur = x_ref[...] * (1.0 + sc) + sh
    buf_ref[0:POOL_HALO, :] = jnp.where(first, 0.0, xp_ref[...] * (1.0 + sc) + sh)
    buf_ref[POOL_HALO:POOL_HALO + tm, :] = cur
    buf_ref[POOL_HALO + tm:2 * POOL_HALO + tm, :] = jnp.where(last, 0.0, xn_ref[...] * (1.0 + sc) + sh)

    in_ctx = i < nct
    t_loc = (lax.broadcasted_iota(jnp.int32, (tm, 1), 0)
             + (i - jnp.where(in_ctx, 0, nct)) * tm)
    n_s = jnp.where(in_ctx, lc, n_lat)
    cdim = w_ref.shape[1]
    for g, win in enumerate(POOL_WINDOWS):
        cols = slice(g * cdim, (g + 1) * cdim)
        half = win // 2
        acc = buf_ref[POOL_HALO - half:POOL_HALO - half + tm, cols]
        for k in range(-half + 1, half):
            acc = acc + buf_ref[POOL_HALO + k:POOL_HALO + k + tm, cols]
        cnt = jnp.minimum(t_loc + half - 1, n_s - 1) - jnp.maximum(t_loc - half, 0) + 1
        y = acc / cnt.astype(F32) - cur[:, cols]
        out = jnp.dot(y.astype(BF16), w_ref[g], preferred_element_type=F32) + b_ref[g]
        o_ref[:, cols] = out * ps_ref[:, cols]


def _pool_mixer(x, sc, sh, pool_w, pool_b, pool_scale, *, lc):
    t_all, d = x.shape
    n_lat = t_all - lc
    ng, cdim, _ = pool_w.shape
    tm = _tile(math.gcd(lc, n_lat), 128)
    hb = tm // POOL_HALO
    n_hb = t_all // POOL_HALO
    stream = lambda i: (jnp.where(i < lc // tm, 0, 1), 0, 0)
    return pl.pallas_call(
        functools.partial(_pool_kernel, lc=lc, n_lat=n_lat, tm=tm),
        grid=(t_all // tm,),
        in_specs=[pl.BlockSpec((tm, d), lambda i: (i, 0)),
                  pl.BlockSpec((POOL_HALO, d), lambda i: (jnp.maximum(i * hb - 1, 0), 0)),
                  pl.BlockSpec((POOL_HALO, d), lambda i: (jnp.minimum((i + 1) * hb, n_hb - 1), 0)),
                  pl.BlockSpec((1, 1, d), stream),
                  pl.BlockSpec((1, 1, d), stream),
                  pl.BlockSpec((ng, cdim, cdim), lambda i: (0, 0, 0)),
                  pl.BlockSpec((ng, 1, cdim), lambda i: (0, 0, 0)),
                  pl.BlockSpec((1, d), lambda i: (0, 0))],
        out_specs=pl.BlockSpec((tm, d), lambda i: (i, 0)),
        out_shape=jax.ShapeDtypeStruct((t_all, d), F32),
        scratch_shapes=[pltpu.VMEM((tm + 2 * POOL_HALO, d), F32)],
        compiler_params=_params("parallel"),
        name="pool_mixer",
    )(x, x, x, sc, sh, pool_w.astype(BF16), pool_b.reshape(ng, 1, cdim), pool_scale.reshape(1, d))


def _moe_kernel(texp_ref, rtok_ref, rdst_ref, nact_ref,
                h_hbm, gate_ref, w1_ref, w3_ref, w2_ref, y_hbm,
                xbuf, ybuf, sem_in, sem_out, *, tm, n_tiles):
    i = pl.program_id(0)
    n = nact_ref[0]
    slot = lax.rem(i, 2)
    spare = y_hbm.shape[0] - 2 * tm

    def gather(tile, s, k):
        tok = rtok_ref[tile * tm + k]
        return pltpu.make_async_copy(h_hbm.at[pl.ds(tok, 1)], xbuf.at[s, pl.ds(k, 1)], sem_in.at[s])

    def scatter(tile, s, k):
        dst = rdst_ref[(tile + 1) * tm + k]
        return pltpu.make_async_copy(ybuf.at[s, pl.ds(k, 1)], y_hbm.at[pl.ds(dst, 1)], sem_out.at[s])

    def fill(k):
        return pltpu.make_async_copy(ybuf.at[0, pl.ds(k, 1)], y_hbm.at[pl.ds(spare + k, 1)], sem_out.at[0])

    @pl.when(i == 0)
    def _():
        ybuf[...] = jnp.zeros_like(ybuf)
        for k in range(tm):
            fill(k).start()
        for k in range(tm):
            gather(0, 0, k).start()
        for k in range(tm):
            fill(k).wait()

    @pl.when(jnp.logical_and(i >= 1, i < n))
    def _():
        for k in range(tm):
            scatter(i - 2, slot, k).wait()

    @pl.when(i < n)
    def _():
        for k in range(tm):
            gather(i, slot, k).wait()
        nxt = jnp.minimum(i + 1, n_tiles - 1)
        x = _unpack_pairs(xbuf[slot]).astype(BF16)
        for k in range(tm):
            gather(nxt, 1 - slot, k).start()
        a = jnp.dot(x, w1_ref[0], preferred_element_type=F32)
        u = jnp.dot(x, w3_ref[0], preferred_element_type=F32)
        for k in range(tm):
            scatter(i - 1, 1 - slot, k).start(priority=1)
        hid = a * _sigmoid(a) * u * gate_ref[...]
        ybuf[slot] = _pack_pairs(jnp.dot(hid.astype(BF16), w2_ref[0], preferred_element_type=F32))

        @pl.when(i == n - 1)
        def _():
            for k in range(tm):
                scatter(i, slot, k).start()
            for k in range(tm):
                gather(nxt, 1 - slot, k).wait()
            for k in range(tm):
                scatter(i - 1, 1 - slot, k).wait()
            for k in range(tm):
                scatter(i, slot, k).wait()


def _moe(h, ids, wts, w1, w3, w2, layer, *, tm_target=256):
    _, n_exp, d, f = w1.shape
    t_all = h.shape[0]
    tm = _tile(t_all, tm_target)
    n_pairs = 2 * t_all
    n_tiles = n_pairs // tm + n_exp
    n_rows = n_tiles * tm

    e_flat = ids.reshape(n_pairs)
    w_flat = wts.reshape(n_pairs)
    onehot = (e_flat[:, None] == jnp.arange(n_exp, dtype=jnp.int32)[None, :]).astype(jnp.int32)
    rank = jnp.sum((jnp.cumsum(onehot, axis=0) - onehot) * onehot, axis=1)
    counts = jnp.sum(onehot, axis=0)
    padded = (counts + tm - 1) // tm * tm
    p_end = jnp.cumsum(padded)
    pos = jnp.sum(onehot * (p_end - padded)[None, :], axis=1) + rank
    row_pair = jnp.full((n_rows,), -1, jnp.int32).at[pos].set(jnp.arange(n_pairs, dtype=jnp.int32))
    valid = row_pair >= 0
    safe = jnp.maximum(row_pair, 0)
    row_tok = jnp.where(valid, safe % t_all, 0)
    row_gate = jnp.where(valid, w_flat[safe], 0.0).reshape(n_rows, 1)
    rows = jnp.arange(-tm, n_rows, dtype=jnp.int32)
    spare_row = n_pairs + (rows // tm) % 2 * tm + rows % tm
    row_dst = jnp.where(rows >= 0, jnp.where(jnp.concatenate([jnp.zeros((tm,), bool), valid]),
                                             jnp.concatenate([jnp.zeros((tm,), jnp.int32), row_pair]),
                                             spare_row), spare_row)
    n_act = (p_end[-1] // tm).astype(jnp.int32)
    tile_start = jnp.arange(n_tiles, dtype=jnp.int32) * tm
    tile_exp = jnp.sum(p_end[None, :] <= tile_start[:, None], axis=1, dtype=jnp.int32)
    last_exp = jnp.sum(p_end <= (n_act - 1) * tm, dtype=jnp.int32)
    tile_exp = jnp.minimum(jnp.where(tile_start < n_act * tm, tile_exp, last_exp), n_exp - 1)

    wspec = lambda shape: pl.BlockSpec((None,) + shape, lambda i, te, rt, rd, na: (layer, te[i], 0, 0))
    return pl.pallas_call(
        functools.partial(_moe_kernel, tm=tm, n_tiles=n_tiles),
        grid_spec=pltpu.PrefetchScalarGridSpec(
            num_scalar_prefetch=4,
            grid=(n_tiles,),
            in_specs=[pl.BlockSpec(memory_space=pl.ANY),
                      pl.BlockSpec((tm, 1), lambda i, te, rt, rd, na: (i, 0)),
                      wspec((1, d, f)), wspec((1, d, f)), wspec((1, f, d))],
            out_specs=pl.BlockSpec(memory_space=pl.ANY),
            scratch_shapes=[pltpu.VMEM((2, tm, d // 2), jnp.uint32)] * 2 + [
                            pltpu.SemaphoreType.DMA((2,)), pltpu.SemaphoreType.DMA((2,))],
        ),
        out_shape=jax.ShapeDtypeStruct((n_pairs + 2 * tm, d // 2), jnp.uint32),
        compiler_params=_params("arbitrary"),
        name="moe_experts",
    )(tile_exp, row_tok, row_dst, n_act.reshape(1), h, row_gate,
      w1, w3, w2)


def _router_weights(rg_w, rg_b, re_w, re_b):
    d = rg_w.shape[0]
    n = rg_w.shape[1] + re_w.shape[1]
    w = jnp.concatenate([rg_w, re_w, jnp.zeros((d, LANES - n), F32)], axis=1)
    b = jnp.concatenate([rg_b, re_b, jnp.zeros((LANES - n,), F32)]).reshape(1, LANES)
    w_hi = w.astype(BF16)
    w_lo = (w - w_hi.astype(F32)).astype(BF16)
    return w_hi, w_lo, b


def kernel(x, c, ctx, c_ctx, ada_w, ada_b, ln_g, ln_b, attn_wq, attn_wk, attn_wv, attn_wo, attn_qn_g, attn_kn_g, pool_w, pool_b, pool_scale, diff_wq, diff_wk, diff_wv, diff_wo, diff_lq1, diff_lk1, diff_lq2, diff_lk2, diff_subln_g, moe_rg_w, moe_rg_b, moe_re_w, moe_re_b, moe_w1, moe_w3, moe_w2):
    b, n, d = x.shape
    assert b == 1 and c.shape[0] == 1 and ctx.shape[0] == 1
    lc = ctx.shape[1]
    depth = ada_w.shape[0]
    alpha = (2 * depth) ** 0.25
    qk_scale = math.log2(math.e) / math.sqrt(HEAD_DIM)

    s = jnp.concatenate([ctx[0], x[0]], axis=0)
    mods = _ada_all(jnp.concatenate([c_ctx[None, :], c], axis=0), ada_w, ada_b)
    mod = lambda i, k: mods[i, :, k * d:(k + 1) * d].reshape(2, 1, d)
    rope = _rope_tables(n, lc)
    bf = lambda w: w.astype(BF16)
    attn_w = [bf(w) for w in (attn_wq, attn_wk, attn_wv, attn_wo)]
    diff_w = [bf(w) for w in (diff_wq, diff_wk, diff_wv, diff_wo)]
    moe_w = [bf(w) for w in (moe_w1, moe_w3, moe_w2)]

    h1 = None
    for i in range(depth):
        last = i == depth - 1
        kind, j = i % N_MIXERS, i // N_MIXERS
        if kind != 1 and h1 is None:
            (h1,) = _ln_mod(s, (), None, None, None, mod(i, 1), mod(i, 0), None,
                            lc=lc, alpha=alpha, out_h=BF16)
        if kind == 0:
            q = _proj(h1, attn_w[0], j, out_dtype=BF16, norm_g=attn_qn_g[j], rope=rope, scale=qk_scale)
            k = _proj(h1, attn_w[1], j, out_dtype=BF16, norm_g=attn_kn_g[j], rope=rope)
            v = _proj(h1, attn_w[2], j, out_dtype=BF16)
            o = _gqa_attention(q, k, v, lc=lc)
            y = _proj(o, attn_w[3], j, out_dtype=F32)
        elif kind == 1:
            y = _pool_mixer(s, mod(i, 1), mod(i, 0), pool_w[j], pool_b[j], pool_scale[j], lc=lc)
        else:
            lam_init = 0.8 - 0.6 * math.exp(-0.3 * i)
            q = _proj(h1, diff_w[0], j, out_dtype=BF16, rope=rope, scale=qk_scale)
            k = _proj(h1, diff_w[1], j, out_dtype=BF16, rope=rope)
            v = _proj(h1, diff_w[2], j, out_dtype=BF16)
            o = _diff_attention(q, k, v, diff_lq1[j], diff_lk1[j], diff_lq2[j], diff_lk2[j],
                                diff_subln_g[j], lc=lc, lam_init=lam_init)
            y = _proj(o, diff_w[3], j, out_dtype=F32)
        h1 = None
        route_w = _router_weights(moe_rg_w[i], moe_rg_b[i], moe_re_w[i], moe_re_b[i])
        s, h2, ids, wts = _ln_mod(s, (y,), mod(i, 2), ln_g[i, 0], ln_b[i, 0], mod(i, 4), mod(i, 3),
                                  route_w, lc=lc, alpha=alpha, out_h=F32, packed_h=True)
        y2 = _moe(h2, ids[:2], wts[:2], *moe_w, i)
        if last:
            (s,) = _ln_mod(s, (y2,), mod(i, 5), ln_g[i, 1], ln_b[i, 1], None, None, None,
                           lc=lc, alpha=alpha, out_h=None, row_offset=lc, packed_y=True)
        elif (i + 1) % N_MIXERS == 1:
            (s,) = _ln_mod(s, (y2,), mod(i, 5), ln_g[i, 1], ln_b[i, 1], None, None, None,
                           lc=lc, alpha=alpha, out_h=None, packed_y=True)
        else:
            s, h1 = _ln_mod(s, (y2,), mod(i, 5), ln_g[i, 1], ln_b[i, 1], mod(i + 1, 1), mod(i + 1, 0),
                            None, lc=lc, alpha=alpha, out_h=BF16, packed_y=True)
    return s[None]
```

```python
import functools
import math

import jax
import jax.numpy as jnp
from jax import lax
from jax.experimental import pallas as pl
from jax.experimental.pallas import tpu as pltpu

HEAD_DIM = 128
GRID_W = 64
ROPE_THETA = 10000.0
POOL_WINDOWS = (2, 4, 8, 16)
POOL_HALO = 8
N_EXPERT_GROUPS = 4
EXPERTS_PER_GROUP = 4
N_EXPERTS = N_EXPERT_GROUPS * EXPERTS_PER_GROUP
N_MIXERS = 3
GQA_GROUP = 4
LN_EPS = 1e-6
RMS_EPS = 1e-6
LANES = 128
SUBLANES = 8
VMEM_LIMIT_BYTES = 56 * 1024 * 1024

F32 = jnp.float32
BF16 = jnp.bfloat16


def _params(*sem):
    return pltpu.CompilerParams(dimension_semantics=sem, vmem_limit_bytes=VMEM_LIMIT_BYTES)


def _tile(n, target, mult=SUBLANES):
    best = None
    for t in range(mult, min(n, target) + 1, mult):
        if n % t == 0:
            best = t
    assert best is not None, (n, target, mult)
    return best


def _sigmoid(v):
    return 1.0 / (1.0 + jnp.exp(-v))


def _pack_pairs(v):
    half = v.shape[1] // 2
    lo = lax.bitcast_convert_type(v[:, :half].astype(BF16).astype(F32), jnp.uint32)
    hi = lax.bitcast_convert_type(v[:, half:].astype(BF16).astype(F32), jnp.uint32)
    return hi | (lo >> 16)


def _unpack_pairs(u):
    lo = lax.bitcast_convert_type(u << 16, F32)
    hi = lax.bitcast_convert_type(u & jnp.uint32(0xFFFF0000), F32)
    return jnp.concatenate([lo, hi], axis=1)


def _ada_kernel(c_ref, w_ref, b_ref, o_ref, acc_ref, *, tk, tn):
    k = pl.program_id(2)

    @pl.when(k == 0)
    def _():
        acc_ref[...] = jnp.zeros_like(acc_ref)

    cv = c_ref[...]
    sv = cv * _sigmoid(cv)
    s0, s1 = sv[0], sv[1]
    for j in range(tn // LANES):
        cols = slice(j * LANES, (j + 1) * LANES)
        wj = w_ref[0, :, cols]
        acc_ref[0, :, cols] += (wj * s0).reshape(tk // SUBLANES, SUBLANES, LANES).sum(axis=0)
        acc_ref[1, :, cols] += (wj * s1).reshape(tk // SUBLANES, SUBLANES, LANES).sum(axis=0)

    @pl.when(k == pl.num_programs(2) - 1)
    def _():
        o_ref[0] = acc_ref[...].sum(axis=1) + b_ref[0]


def _ada_all(cvec, ada_w, ada_b):
    n_layers, d, n_out = ada_w.shape
    tk = _tile(d, 1024)
    tn = _tile(n_out, 2048, LANES)
    c_rep = jnp.broadcast_to(cvec[:, :, None], (2, d, LANES))
    return pl.pallas_call(
        functools.partial(_ada_kernel, tk=tk, tn=tn),
        grid=(n_layers, n_out // tn, d // tk),
        in_specs=[
            pl.BlockSpec((2, tk, LANES), lambda l, j, k: (0, k, 0)),
            pl.BlockSpec((1, tk, tn), lambda l, j, k: (l, k, j)),
            pl.BlockSpec((1, 1, tn), lambda l, j, k: (l, 0, j)),
        ],
        out_specs=pl.BlockSpec((1, 2, tn), lambda l, j, k: (l, 0, j)),
        out_shape=jax.ShapeDtypeStruct((n_layers, 2, n_out), F32),
        scratch_shapes=[pltpu.VMEM((2, SUBLANES, tn), F32)],
        compiler_params=_params("parallel", "parallel", "arbitrary"),
        name="ada_mod",
    )(c_rep, ada_w, ada_b.reshape(n_layers, 1, n_out))


def _route_rows(lg):
    ng, ne = N_EXPERT_GROUPS, EXPERTS_PER_GROUP
    g = [lg[r:r + 1, :] for r in range(ng)]
    gmax = functools.reduce(jnp.maximum, g)
    gidx = jnp.full(gmax.shape, ng - 1, jnp.int32)
    for r in range(ng - 2, -1, -1):
        gidx = jnp.where(g[r] == gmax, r, gidx)
    p_g = 1.0 / functools.reduce(jnp.add, [jnp.exp(gr - gmax) for gr in g])
    sel = []
    for j in range(ne):
        v = lg[ng + (ng - 1) * ne + j:ng + (ng - 1) * ne + j + 1, :]
        for r in range(ng - 2, -1, -1):
            v = jnp.where(gidx == r, lg[ng + r * ne + j:ng + r * ne + j + 1, :], v)
        sel.append(v)
    v1 = functools.reduce(jnp.maximum, sel)
    i1 = jnp.full(v1.shape, ne - 1, jnp.int32)
    for j in range(ne - 2, -1, -1):
        i1 = jnp.where(sel[j] == v1, j, i1)
    rest = [jnp.where(i1 == j, -jnp.inf, sel[j]) for j in range(ne)]
    v2 = functools.reduce(jnp.maximum, rest)
    i2 = jnp.full(v2.shape, ne - 1, jnp.int32)
    for j in range(ne - 2, -1, -1):
        i2 = jnp.where(rest[j] == v2, j, i2)
    t = jnp.exp(v2 - v1)
    w1 = p_g / (1.0 + t)
    w2 = p_g * t / (1.0 + t)
    ids = jnp.concatenate([gidx * ne + i1, gidx * ne + i2], axis=0)
    wts = jnp.concatenate([w1, w2], axis=0)
    return ids, wts


def _ln_mod_kernel(*refs, n_y, do_ln, out_h, do_route, alpha, packed_y, packed_h):
    refs = list(refs)
    x_ref = refs.pop(0)
    y_refs = [refs.pop(0) for _ in range(n_y)]
    if do_ln:
        gate_ref, lng_ref, lnb_ref = refs.pop(0), refs.pop(0), refs.pop(0)
    if out_h is not None:
        sc_ref, sh_ref = refs.pop(0), refs.pop(0)
    if do_route:
        whi_ref, wlo_ref, rb_ref = refs.pop(0), refs.pop(0), refs.pop(0)
    xo_ref = refs.pop(0) if do_ln else None
    h_ref = refs.pop(0) if out_h is not None else None
    if do_route:
        ids_ref, wts_ref = refs.pop(0), refs.pop(0)
    assert not refs

    x = x_ref[...]
    if do_ln:
        y = functools.reduce(jnp.add, [_unpack_pairs(r[...]) if packed_y else r[...] for r in y_refs])
        z = alpha * x + gate_ref[0] * y
        mu = jnp.mean(z, axis=-1, keepdims=True)
        zc = z - mu
        var = jnp.mean(zc * zc, axis=-1, keepdims=True)
        x = zc * lax.rsqrt(var + LN_EPS) * lng_ref[...] + lnb_ref[...]
        xo_ref[...] = x
    if out_h is not None:
        h = x * (1.0 + sc_ref[0]) + sh_ref[0]
        h_ref[...] = _pack_pairs(h) if packed_h else h.astype(out_h)
    if do_route:
        h_hi = h.astype(BF16)
        h_lo = (h - h_hi.astype(F32)).astype(BF16)
        w_hi = whi_ref[...]
        lg = (jnp.dot(h_hi, w_hi, preferred_element_type=F32)
              + jnp.dot(h_lo, w_hi, preferred_element_type=F32)
              + jnp.dot(h_hi, wlo_ref[...], preferred_element_type=F32)) + rb_ref[...]
        ids, wts = _route_rows(lg.T)
        pad = SUBLANES - ids.shape[0]
        ids_ref[...] = jnp.concatenate([ids, jnp.zeros((pad, ids.shape[1]), jnp.int32)], axis=0)
        wts_ref[...] = jnp.concatenate([wts, jnp.zeros((pad, wts.shape[1]), F32)], axis=0)


def _ln_mod(x, ys, gate, ln_g, ln_b, sc, sh, route_w, *, lc, alpha, out_h, row_offset=0,
            packed_y=False, packed_h=False):
    t_all, d = x.shape
    do_ln = len(ys) > 0
    do_route = route_w is not None
    tm = _tile(math.gcd(lc, t_all - lc), 256)
    assert row_offset % tm == 0
    off = row_offset // tm
    t_out = t_all - row_offset
    nct = lc // tm

    row = lambda i: (i + off, 0)
    stream = lambda i: (jnp.where(i + off < nct, 0, 1), 0, 0)
    vec = pl.BlockSpec((1, d), lambda i: (0, 0))
    svec = pl.BlockSpec((1, 1, d), stream)

    args, in_specs = [x], [pl.BlockSpec((tm, d), row)]
    n_y = 0
    for y in ys:
        for grp in range(y.shape[0] // t_all):
            n_y += 1
            args.append(y)
            in_specs.append(pl.BlockSpec((tm, y.shape[1]), lambda i, grp=grp: (i + off + grp * (t_all // tm), 0)))
    if do_ln:
        args += [gate, ln_g.reshape(1, d), ln_b.reshape(1, d)]
        in_specs += [svec, vec, vec]
    if out_h is not None:
        args += [sc, sh]
        in_specs += [svec, svec]
    if do_route:
        args += list(route_w)
        in_specs += [pl.BlockSpec((d, LANES), lambda i: (0, 0)),
                     pl.BlockSpec((d, LANES), lambda i: (0, 0)),
                     pl.BlockSpec((1, LANES), lambda i: (0, 0))]
    out_shape, out_specs = [], []
    if do_ln:
        out_shape.append(jax.ShapeDtypeStruct((t_out, d), F32))
        out_specs.append(pl.BlockSpec((tm, d), lambda i: (i, 0)))
    if out_h is not None:
        hd, hdt = (d // 2, jnp.uint32) if packed_h else (d, out_h)
        out_shape.append(jax.ShapeDtypeStruct((t_out, hd), hdt))
        out_specs.append(pl.BlockSpec((tm, hd), lambda i: (i, 0)))
    if do_route:
        out_shape += [jax.ShapeDtypeStruct((SUBLANES, t_out), jnp.int32),
                      jax.ShapeDtypeStruct((SUBLANES, t_out), F32)]
        out_specs += [pl.BlockSpec((SUBLANES, tm), lambda i: (0, i))] * 2
    return pl.pallas_call(
        functools.partial(_ln_mod_kernel, n_y=n_y, do_ln=do_ln, out_h=out_h,
                          do_route=do_route, alpha=alpha, packed_y=packed_y, packed_h=packed_h),
        grid=(t_out // tm,),
        in_specs=in_specs,
        out_specs=out_specs,
        out_shape=out_shape,
        compiler_params=_params("parallel"),
        name="ln_mod",
    )(*args)


def _proj_kernel(*refs, norm, rope, scale, tn):
    refs = list(refs)
    a_ref, w_ref = refs.pop(0), refs.pop(0)
    g_ref = refs.pop(0) if norm else None
    if rope:
        cos_ref, sin_ref = refs.pop(0), refs.pop(0)
    o_ref = refs.pop(0)
    acc = jnp.dot(a_ref[...], w_ref[...], preferred_element_type=F32)
    if not (norm or rope):
        if scale != 1.0:
            acc = acc * scale
        o_ref[...] = acc.astype(o_ref.dtype)
        return
    if rope:
        cos, sin = cos_ref[...], sin_ref[...]
        lane = lax.broadcasted_iota(jnp.int32, cos.shape, 1)
        first_half = (lane % (HEAD_DIM // 2)) < (HEAD_DIM // 4)
    for hh in range(tn // HEAD_DIM):
        cols = slice(hh * HEAD_DIM, (hh + 1) * HEAD_DIM)
        t = acc[:, cols]
        if norm:
            t = t * lax.rsqrt(jnp.mean(t * t, axis=-1, keepdims=True) + RMS_EPS) * g_ref[...]
        if rope:
            up = pltpu.roll(t, HEAD_DIM - HEAD_DIM // 4, 1)
            dn = pltpu.roll(t, HEAD_DIM // 4, 1)
            t = t * cos + jnp.where(first_half, up, dn) * sin
        if scale != 1.0:
            t = t * scale
        o_ref[:, cols] = t.astype(o_ref.dtype)


def _proj(a, w, layer, *, out_dtype, norm_g=None, rope=None, scale=1.0, tm_target=768, tn_target=1024):
    t_all, kdim = a.shape
    n = w.shape[2]
    tm = _tile(t_all, tm_target)
    tn = _tile(n, tn_target, LANES)
    args = [a, w]
    in_specs = [pl.BlockSpec((tm, kdim), lambda i, j: (i, 0)),
                pl.BlockSpec((None, kdim, tn), lambda i, j: (layer, 0, j))]
    if norm_g is not None:
        args.append(norm_g.reshape(1, HEAD_DIM))
        in_specs.append(pl.BlockSpec((1, HEAD_DIM), lambda i, j: (0, 0)))
    if rope is not None:
        args += list(rope)
        in_specs += [pl.BlockSpec((tm, HEAD_DIM), lambda i, j: (i, 0))] * 2
    return pl.pallas_call(
        functools.partial(_proj_kernel, norm=norm_g is not None, rope=rope is not None,
                          scale=scale, tn=tn),
        grid=(t_all // tm, n // tn),
        in_specs=in_specs,
        out_specs=pl.BlockSpec((tm, tn), lambda i, j: (i, j)),
        out_shape=jax.ShapeDtypeStruct((t_all, n), out_dtype),
        compiler_params=_params("parallel", "parallel"),
        name="proj",
    )(*args)


def _proj_rope_kernel(*refs, norm, scale, tn):
    refs = list(refs)
    a_ref, w_ref = refs.pop(0), refs.pop(0)
    g_ref = refs.pop(0) if norm else None
    cos_ref, sin_ref, o_ref, acc0, acc1 = refs
    s = pl.program_id(0)

    @pl.when(s == 0)
    def _():
        acc1[...] = jnp.zeros_like(acc1)

    def epilogue(acc_ref):
        cos, sin = cos_ref[...], sin_ref[...]
        lane = lax.broadcasted_iota(jnp.int32, cos.shape, 1)
        first_half = (lane % (HEAD_DIM // 2)) < (HEAD_DIM // 4)
        for hh in range(tn // HEAD_DIM):
            cols = slice(hh * HEAD_DIM, (hh + 1) * HEAD_DIM)
            t = acc_ref[:, cols]
            if norm:
                t = t * lax.rsqrt(jnp.mean(t * t, axis=-1, keepdims=True) + RMS_EPS) * g_ref[...]
            up = pltpu.roll(t, HEAD_DIM - HEAD_DIM // 4, 1)
            dn = pltpu.roll(t, HEAD_DIM // 4, 1)
            t = t * cos + jnp.where(first_half, up, dn) * sin
            if scale != 1.0:
                t = t * scale
            o_ref[:, cols] = t.astype(o_ref.dtype)

    def step(dst, src):
        dst[...] = jnp.dot(a_ref[...], w_ref[...], preferred_element_type=F32)
        epilogue(src)

    pl.when(s % 2 == 0)(lambda: step(acc0, acc1))
    pl.when(s % 2 == 1)(lambda: step(acc1, acc0))


def _proj_rope(a, w, layer, rope, *, norm_g=None, scale=1.0, tm_target=768, tn_target=1024):
    t_all, kdim = a.shape
    n = w.shape[2]
    tm = _tile(t_all, tm_target)
    tn = _tile(n, tn_target, LANES)
    n_j = n // tn
    n_tiles = (t_all // tm) * n_j
    mm = lambda s: jnp.minimum(s, n_tiles - 1)
    ep = lambda s: jnp.maximum(s - 1, 0)
    args = [a, w]
    in_specs = [pl.BlockSpec((tm, kdim), lambda s: (mm(s) // n_j, 0)),
                pl.BlockSpec((None, kdim, tn), lambda s: (layer, 0, mm(s) % n_j))]
    if norm_g is not None:
        args.append(norm_g.reshape(1, HEAD_DIM))
        in_specs.append(pl.BlockSpec((1, HEAD_DIM), lambda s: (0, 0)))
    args += list(rope)
    in_specs += [pl.BlockSpec((tm, HEAD_DIM), lambda s: (ep(s) // n_j, 0))] * 2
    return pl.pallas_call(
        functools.partial(_proj_rope_kernel, norm=norm_g is not None, scale=scale, tn=tn),
        grid=(n_tiles + 1,),
        in_specs=in_specs,
        out_specs=pl.BlockSpec((tm, tn), lambda s: (ep(s) // n_j, ep(s) % n_j)),
        out_shape=jax.ShapeDtypeStruct((t_all, n), BF16),
        scratch_shapes=[pltpu.VMEM((tm, tn), F32)] * 2,
        compiler_params=_params("arbitrary"),
        name="proj_rope",
    )(*args)


def _rope_tables(n, lc):
    n_freq = HEAD_DIM // 4
    pos = jnp.arange(n, dtype=jnp.int32)
    row = (pos // GRID_W).astype(F32)
    col = (pos % GRID_W).astype(F32)
    inv_freq = ROPE_THETA ** (-jnp.arange(n_freq, dtype=F32) / n_freq)
    ar, ac = row[:, None] * inv_freq, col[:, None] * inv_freq
    cos = jnp.concatenate([jnp.cos(ar), jnp.cos(ar), jnp.cos(ac), jnp.cos(ac)], axis=1)
    sin = jnp.concatenate([-jnp.sin(ar), jnp.sin(ar), -jnp.sin(ac), jnp.sin(ac)], axis=1)
    cos = jnp.concatenate([jnp.ones((lc, HEAD_DIM), F32), cos], axis=0)
    sin = jnp.concatenate([jnp.zeros((lc, HEAD_DIM), F32), sin], axis=0)
    return cos, sin


def _sweep(qs, kcols, k_ref, v_ref, latent_rows, finish, *, lc, tk, n_lat, dv, sum_on_mxu):
    nq = len(qs)
    n = n_lat // tk

    def e0(size):
        return (lax.broadcasted_iota(jnp.int32, (size, LANES), 1) == 0).astype(v_ref.dtype)

    ones_c, ones_t = (e0(lc), e0(tk)) if sum_on_mxu else (None, None)

    def scores(lo, size):
        return [lax.dot_general(q, k_ref[pl.ds(lo, size), kc], (((1,), (1,)), ((), ())),
                                preferred_element_type=F32) for q, kc in zip(qs, kcols)]

    def pv(p, lo, size, ones):
        vc = v_ref[pl.ds(lo, size), :]
        if sum_on_mxu:
            vc = jnp.concatenate([vc, ones], axis=1)
        return jnp.dot(p, vc, preferred_element_type=F32)

    def soft(s, m, l):
        m_new = jnp.maximum(m, jnp.max(s, axis=-1, keepdims=True))
        a = jnp.exp2(m - m_new)
        p = jnp.exp2(s - m_new)
        if not sum_on_mxu:
            l = a * l + jnp.sum(p, axis=-1, keepdims=True)
        return m_new, a, p.astype(v_ref.dtype), l

    def result(acc, l):
        return acc[:, :dv] / acc[:, dv:dv + 1] if sum_on_mxu else acc / l

    def context():
        out = []
        for s in scores(0, lc):
            m = jnp.max(s, axis=-1, keepdims=True)
            p = jnp.exp2(s - m)
            l = None if sum_on_mxu else jnp.sum(p, axis=-1, keepdims=True)
            out.append((m, l, pv(p.astype(v_ref.dtype), 0, lc, ones_c)))
        return out

    @pl.when(jnp.logical_not(latent_rows))
    def _():
        finish([result(acc, l) for _, l, acc in context()])

    @pl.when(latent_rows)
    def _():
        ctx = context()

        def start(c, s):
            m0, l0, acc0 = ctx[c]
            m, a, p, l = soft(s, m0, l0)
            return (m, p, a * acc0) if sum_on_mxu else (m, l, p, a * acc0)

        state = tuple(start(c, s) for c, s in enumerate(scores(lc, tk)))

        def body(j, state):
            lo = pl.multiple_of(lc + j * tk, math.gcd(lc, tk))
            ss = scores(lo, tk)
            new = []
            for c in range(nq):
                m0, l0 = state[c][0], None if sum_on_mxu else state[c][1]
                acc = state[c][-1] + pv(state[c][-2], lo - tk, tk, ones_t)
                m, a, p, l = soft(ss[c], m0, l0)
                new.append((m, p, a * acc) if sum_on_mxu else (m, l, p, a * acc))
            return tuple(new)

        state = lax.fori_loop(1, n, body, state, unroll=True)
        outs = []
        for c in range(nq):
            l = None if sum_on_mxu else state[c][1]
            p, r = state[c][-2], state[c][-1]
            outs.append(result(r + pv(p, lc + (n - 1) * tk, tk, ones_t), l))
        finish(outs)


def _gqa_kernel(q_ref, k_ref, v_ref, o_ref, *, lc, tq, tk, n_lat):
    i = pl.program_id(1)
    heads = [slice(h * HEAD_DIM, (h + 1) * HEAD_DIM) for h in range(GQA_GROUP)]

    def finish(outs):
        for cols, o in zip(heads, outs):
            o_ref[:, cols] = o.astype(o_ref.dtype)

    _sweep([q_ref[:, cols] for cols in heads], [heads[0]] * GQA_GROUP, k_ref, v_ref, i * tq >= lc, finish,
           lc=lc, tk=tk, n_lat=n_lat, dv=HEAD_DIM, sum_on_mxu=True)


def _gqa_attention(q, k, v, *, lc, tq_target=256, tk_target=2048):
    t_all, dq = q.shape
    n_kv = k.shape[1] // HEAD_DIM
    n_lat = t_all - lc
    tq = _tile(math.gcd(lc, n_lat), tq_target)
    tk = _tile(n_lat, tk_target)
    gw = GQA_GROUP * HEAD_DIM
    return pl.pallas_call(
        functools.partial(_gqa_kernel, lc=lc, tq=tq, tk=tk, n_lat=n_lat),
        grid=(n_kv, t_all // tq),
        in_specs=[pl.BlockSpec((tq, gw), lambda g, i: (i, g)),
                  pl.BlockSpec((t_all, HEAD_DIM), lambda g, i: (0, g)),
                  pl.BlockSpec((t_all, HEAD_DIM), lambda g, i: (0, g))],
        out_specs=pl.BlockSpec((tq, gw), lambda g, i: (i, g)),
        out_shape=jax.ShapeDtypeStruct((t_all, dq), BF16),
        compiler_params=_params("parallel", "parallel"),
        name="gqa_attention",
    )(q, k, v)


def _diff_kernel(q_ref, k_ref, v_ref, lq1_ref, lk1_ref, lq2_ref, lk2_ref, g_ref, o_ref,
                 *, lc, tq, tk, n_lat, lam_init):
    i = pl.program_id(1)
    dv = 2 * HEAD_DIM

    def finish(outs):
        lam = (jnp.exp(jnp.sum(lq1_ref[...] * lk1_ref[...], axis=-1, keepdims=True))
               - jnp.exp(jnp.sum(lq2_ref[...] * lk2_ref[...], axis=-1, keepdims=True)) + lam_init)
        o = outs[0] - lam * outs[1]
        o = o * lax.rsqrt(jnp.mean(o * o, axis=-1, keepdims=True) + RMS_EPS) * g_ref[...]
        o_ref[...] = (o * (1.0 - lam_init)).astype(o_ref.dtype)

    halves = [slice(0, HEAD_DIM), slice(HEAD_DIM, dv)]
    _sweep([q_ref[:, cols] for cols in halves], halves, k_ref, v_ref, i * tq >= lc, finish,
           lc=lc, tk=tk, n_lat=n_lat, dv=dv, sum_on_mxu=False)


def _diff_attention(q, k, v, lq1, lk1, lq2, lk2, subln_g, *, lc, lam_init, tq_target=256, tk_target=1024):
    t_all, dq = q.shape
    dv = 2 * HEAD_DIM
    n_heads = dq // dv
    n_lat = t_all - lc
    tq = _tile(math.gcd(lc, n_lat), tq_target)
    tk = _tile(n_lat, tk_target)
    vec = lambda n: pl.BlockSpec((1, n), lambda h, i: (0, 0))
    return pl.pallas_call(
        functools.partial(_diff_kernel, lc=lc, tq=tq, tk=tk, n_lat=n_lat, lam_init=lam_init),
        grid=(n_heads, t_all // tq),
        in_specs=[pl.BlockSpec((tq, dv), lambda h, i: (i, h)),
                  pl.BlockSpec((t_all, dv), lambda h, i: (0, h)),
                  pl.BlockSpec((t_all, dv), lambda h, i: (0, h)),
                  vec(HEAD_DIM), vec(HEAD_DIM), vec(HEAD_DIM), vec(HEAD_DIM), vec(dv)],
        out_specs=pl.BlockSpec((tq, dv), lambda h, i: (i, h)),
        out_shape=jax.ShapeDtypeStruct((t_all, v.shape[1]), BF16),
        compiler_params=_params("parallel", "parallel"),
        name="diff_attention",
    )(q, k, v, lq1.reshape(1, -1), lk1.reshape(1, -1), lq2.reshape(1, -1), lk2.reshape(1, -1),
      subln_g.reshape(1, -1))


def _pool_kernel(x_ref, xp_ref, xn_ref, sc_ref, sh_ref, w_ref, b_ref, ps_ref, o_ref, buf_ref,
                 *, lc, n_lat, tm):
    i = pl.program_id(0)
    nct = lc // tm
    nt = (lc + n_lat) // tm
    sc, sh = sc_ref[0], sh_ref[0]
    first = jnp.logical_or(i == 0, i == nct)
    last = jnp.logical_or(i == nct - 1, i == nt - 1)
    cur = x_ref[...] * (1.0 + sc) + sh
    buf_ref[0:POOL_HALO, :] = jnp.where(first, 0.0, xp_ref[...] * (1.0 + sc) + sh)
    buf_ref[POOL_HALO:POOL_HALO + tm, :] = cur
    buf_ref[POOL_HALO + tm:2 * POOL_HALO + tm, :] = jnp.where(last, 0.0, xn_ref[...] * (1.0 + sc) + sh)

    in_ctx = i < nct
    t_loc = (lax.broadcasted_iota(jnp.int32, (tm, 1), 0)
             + (i - jnp.where(in_ctx, 0, nct)) * tm)
    n_s = jnp.where(in_ctx, lc, n_lat)
    cdim = w_ref.shape[1]
    for g, win in enumerate(POOL_WINDOWS):
        cols = slice(g * cdim, (g + 1) * cdim)
        half = win // 2
        acc = buf_ref[POOL_HALO - half:POOL_HALO - half + tm, cols]
        for k in range(-half + 1, half):
            acc = acc + buf_ref[POOL_HALO + k:POOL_HALO + k + tm, cols]
        cnt = jnp.minimum(t_loc + half - 1, n_s - 1) - jnp.maximum(t_loc - half, 0) + 1
        y = acc / cnt.astype(F32) - cur[:, cols]
        out = jnp.dot(y.astype(BF16), w_ref[g], preferred_element_type=F32) + b_ref[g]
        o_ref[:, cols] = out * ps_ref[:, cols]


def _pool_mixer(x, sc, sh, pool_w, pool_b, pool_scale, *, lc):
    t_all, d = x.shape
    n_lat = t_all - lc
    ng, cdim, _ = pool_w.shape
    tm = _tile(math.gcd(lc, n_lat), 128)
    hb = tm // POOL_HALO
    n_hb = t_all // POOL_HALO
    stream = lambda i: (jnp.where(i < lc // tm, 0, 1), 0, 0)
    return pl.pallas_call(
        functools.partial(_pool_kernel, lc=lc, n_lat=n_lat, tm=tm),
        grid=(t_all // tm,),
        in_specs=[pl.BlockSpec((tm, d), lambda i: (i, 0)),
                  pl.BlockSpec((POOL_HALO, d), lambda i: (jnp.maximum(i * hb - 1, 0), 0)),
                  pl.BlockSpec((POOL_HALO, d), lambda i: (jnp.minimum((i + 1) * hb, n_hb - 1), 0)),
                  pl.BlockSpec((1, 1, d), stream),
                  pl.BlockSpec((1, 1, d), stream),
                  pl.BlockSpec((ng, cdim, cdim), lambda i: (0, 0, 0)),
                  pl.BlockSpec((ng, 1, cdim), lambda i: (0, 0, 0)),
                  pl.BlockSpec((1, d), lambda i: (0, 0))],
        out_specs=pl.BlockSpec((tm, d), lambda i: (i, 0)),
        out_shape=jax.ShapeDtypeStruct((t_all, d), F32),
        scratch_shapes=[pltpu.VMEM((tm + 2 * POOL_HALO, d), F32)],
        compiler_params=_params("parallel"),
        name="pool_mixer",
    )(x, x, x, sc, sh, pool_w.astype(BF16), pool_b.reshape(ng, 1, cdim), pool_scale.reshape(1, d))


def _moe_kernel(texp_ref, rtok_ref, rdst_ref, nact_ref,
                h_hbm, gate_ref, w1_ref, w3_ref, w2_ref, y_hbm,
                xbuf, ybuf, sem_in, sem_out, *, tm, n_tiles):
    i = pl.program_id(0)
    n = nact_ref[0]
    slot = lax.rem(i, 2)
    spare = y_hbm.shape[0] - 2 * tm

    def gather(tile, s, k):
        tok = rtok_ref[tile * tm + k]
        return pltpu.make_async_copy(h_hbm.at[pl.ds(tok, 1)], xbuf.at[s, pl.ds(k, 1)], sem_in.at[s])

    def scatter(tile, s, k):
        dst = rdst_ref[(tile + 1) * tm + k]
        return pltpu.make_async_copy(ybuf.at[s, pl.ds(k, 1)], y_hbm.at[pl.ds(dst, 1)], sem_out.at[s])

    def fill(k):
        return pltpu.make_async_copy(ybuf.at[0, pl.ds(k, 1)], y_hbm.at[pl.ds(spare + k, 1)], sem_out.at[0])

    @pl.when(i == 0)
    def _():
        ybuf[...] = jnp.zeros_like(ybuf)
        for k in range(tm):
            fill(k).start()
        for k in range(tm):
            gather(0, 0, k).start()
        for k in range(tm):
            fill(k).wait()

    @pl.when(jnp.logical_and(i >= 1, i < n))
    def _():
        for k in range(tm):
            scatter(i - 2, slot, k).wait()

    @pl.when(i < n)
    def _():
        for k in range(tm):
            gather(i, slot, k).wait()
        nxt = jnp.minimum(i + 1, n_tiles - 1)
        x = _unpack_pairs(xbuf[slot]).astype(BF16)
        for k in range(tm):
            gather(nxt, 1 - slot, k).start()
        a = jnp.dot(x, w1_ref[0], preferred_element_type=F32)
        u = jnp.dot(x, w3_ref[0], preferred_element_type=F32)
        for k in range(tm):
            scatter(i - 1, 1 - slot, k).start(priority=1)
        hid = a * _sigmoid(a) * u * gate_ref[...]
        ybuf[slot] = _pack_pairs(jnp.dot(hid.astype(BF16), w2_ref[0], preferred_element_type=F32))

        @pl.when(i == n - 1)
        def _():
            for k in range(tm):
                scatter(i, slot, k).start()
            for k in range(tm):
                gather(nxt, 1 - slot, k).wait()
            for k in range(tm):
                scatter(i - 1, 1 - slot, k).wait()
            for k in range(tm):
                scatter(i, slot, k).wait()


def _moe(h, ids, wts, w1, w3, w2, layer, *, tm_target=256):
    _, n_exp, d, f = w1.shape
    t_all = h.shape[0]
    tm = _tile(t_all, tm_target)
    n_pairs = 2 * t_all
    n_tiles = n_pairs // tm + n_exp
    n_rows = n_tiles * tm

    e_flat = ids.reshape(n_pairs)
    w_flat = wts.reshape(n_pairs)
    onehot = (e_flat[:, None] == jnp.arange(n_exp, dtype=jnp.int32)[None, :]).astype(jnp.int32)
    rank = jnp.sum((jnp.cumsum(onehot, axis=0) - onehot) * onehot, axis=1)
    counts = jnp.sum(onehot, axis=0)
    padded = (counts + tm - 1) // tm * tm
    p_end = jnp.cumsum(padded)
    pos = jnp.sum(onehot * (p_end - padded)[None, :], axis=1) + rank
    row_pair = jnp.full((n_rows,), -1, jnp.int32).at[pos].set(jnp.arange(n_pairs, dtype=jnp.int32))
    valid = row_pair >= 0
    safe = jnp.maximum(row_pair, 0)
    row_tok = jnp.where(valid, safe % t_all, 0)
    row_gate = jnp.where(valid, w_flat[safe], 0.0).reshape(n_rows, 1)
    rows = jnp.arange(-tm, n_rows, dtype=jnp.int32)
    spare_row = n_pairs + (rows // tm) % 2 * tm + rows % tm
    row_dst = jnp.where(rows >= 0, jnp.where(jnp.concatenate([jnp.zeros((tm,), bool), valid]),
                                             jnp.concatenate([jnp.zeros((tm,), jnp.int32), row_pair]),
                                             spare_row), spare_row)
    n_act = (p_end[-1] // tm).astype(jnp.int32)
    tile_start = jnp.arange(n_tiles, dtype=jnp.int32) * tm
    tile_exp = jnp.sum(p_end[None, :] <= tile_start[:, None], axis=1, dtype=jnp.int32)
    last_exp = jnp.sum(p_end <= (n_act - 1) * tm, dtype=jnp.int32)
    tile_exp = jnp.minimum(jnp.where(tile_start < n_act * tm, tile_exp, last_exp), n_exp - 1)

    wspec = lambda shape: pl.BlockSpec((None,) + shape, lambda i, te, rt, rd, na: (layer, te[i], 0, 0))
    return pl.pallas_call(
        functools.partial(_moe_kernel, tm=tm, n_tiles=n_tiles),
        grid_spec=pltpu.PrefetchScalarGridSpec(
            num_scalar_prefetch=4,
            grid=(n_tiles,),
            in_specs=[pl.BlockSpec(memory_space=pl.ANY),
                      pl.BlockSpec((tm, 1), lambda i, te, rt, rd, na: (i, 0)),
                      wspec((1, d, f)), wspec((1, d, f)), wspec((1, f, d))],
            out_specs=pl.BlockSpec(memory_space=pl.ANY),
            scratch_shapes=[pltpu.VMEM((2, tm, d // 2), jnp.uint32)] * 2 + [
                            pltpu.SemaphoreType.DMA((2,)), pltpu.SemaphoreType.DMA((2,))],
        ),
        out_shape=jax.ShapeDtypeStruct((n_pairs + 2 * tm, d // 2), jnp.uint32),
        compiler_params=_params("arbitrary"),
        name="moe_experts",
    )(tile_exp, row_tok, row_dst, n_act.reshape(1), h, row_gate,
      w1, w3, w2)


def _router_weights(rg_w, rg_b, re_w, re_b):
    d = rg_w.shape[0]
    n = rg_w.shape[1] + re_w.shape[1]
    w = jnp.concatenate([rg_w, re_w, jnp.zeros((d, LANES - n), F32)], axis=1)
    b = jnp.concatenate([rg_b, re_b, jnp.zeros((LANES - n,), F32)]).reshape(1, LANES)
    w_hi = w.astype(BF16)
    w_lo = (w - w_hi.astype(F32)).astype(BF16)
    return w_hi, w_lo, b


def kernel(x, c, ctx, c_ctx, ada_w, ada_b, ln_g, ln_b, attn_wq, attn_wk, attn_wv, attn_wo, attn_qn_g, attn_kn_g, pool_w, pool_b, pool_scale, diff_wq, diff_wk, diff_wv, diff_wo, diff_lq1, diff_lk1, diff_lq2, diff_lk2, diff_subln_g, moe_rg_w, moe_rg_b, moe_re_w, moe_re_b, moe_w1, moe_w3, moe_w2):
    b, n, d = x.shape
    assert b == 1 and c.shape[0] == 1 and ctx.shape[0] == 1
    lc = ctx.shape[1]
    depth = ada_w.shape[0]
    alpha = (2 * depth) ** 0.25
    qk_scale = math.log2(math.e) / math.sqrt(HEAD_DIM)

    s = jnp.concatenate([ctx[0], x[0]], axis=0)
    mods = _ada_all(jnp.concatenate([c_ctx[None, :], c], axis=0), ada_w, ada_b)
    mod = lambda i, k: mods[i, :, k * d:(k + 1) * d].reshape(2, 1, d)
    rope = _rope_tables(n, lc)
    bf = lambda w: w.astype(BF16)
    attn_w = [bf(w) for w in (attn_wq, attn_wk, attn_wv, attn_wo)]
    diff_w = [bf(w) for w in (diff_wq, diff_wk, diff_wv, diff_wo)]
    moe_w = [bf(w) for w in (moe_w1, moe_w3, moe_w2)]

    h1 = None
    for i in range(depth):
        last = i == depth - 1
        kind, j = i % N_MIXERS, i // N_MIXERS
        if kind != 1 and h1 is None:
            (h1,) = _ln_mod(s, (), None, None, None, mod(i, 1), mod(i, 0), None,
                            lc=lc, alpha=alpha, out_h=BF16)
        if kind == 0:
            q = _proj_rope(h1, attn_w[0], j, rope, norm_g=attn_qn_g[j], scale=qk_scale)
            k = _proj_rope(h1, attn_w[1], j, rope, norm_g=attn_kn_g[j])
            v = _proj(h1, attn_w[2], j, out_dtype=BF16)
            o = _gqa_attention(q, k, v, lc=lc)
            y = _proj(o, attn_w[3], j, out_dtype=F32)
        elif kind == 1:
            y = _pool_mixer(s, mod(i, 1), mod(i, 0), pool_w[j], pool_b[j], pool_scale[j], lc=lc)
        else:
            lam_init = 0.8 - 0.6 * math.exp(-0.3 * i)
            q = _proj_rope(h1, diff_w[0], j, rope, scale=qk_scale)
            k = _proj_rope(h1, diff_w[1], j, rope)
            v = _proj(h1, diff_w[2], j, out_dtype=BF16)
            o = _diff_attention(q, k, v, diff_lq1[j], diff_lk1[j], diff_lq2[j], diff_lk2[j],
                                diff_subln_g[j], lc=lc, lam_init=lam_init)
            y = _proj(o, diff_w[3], j, out_dtype=F32)
        h1 = None
        route_w = _router_weights(moe_rg_w[i], moe_rg_b[i], moe_re_w[i], moe_re_b[i])
        s, h2, ids, wts = _ln_mod(s, (y,), mod(i, 2), ln_g[i, 0], ln_b[i, 0], mod(i, 4), mod(i, 3),
                                  route_w, lc=lc, alpha=alpha, out_h=F32, packed_h=True)
        y2 = _moe(h2, ids[:2], wts[:2], *moe_w, i)
        if last:
            (s,) = _ln_mod(s, (y2,), mod(i, 5), ln_g[i, 1], ln_b[i, 1], None, None, None,
                           lc=lc, alpha=alpha, out_h=None, row_offset=lc, packed_y=True)
        elif (i + 1) % N_MIXERS == 1:
            (s,) = _ln_mod(s, (y2,), mod(i, 5), ln_g[i, 1], ln_b[i, 1], None, None, None,
                           lc=lc, alpha=alpha, out_h=None, packed_y=True)
        else:
            s, h1 = _ln_mod(s, (y2,), mod(i, 5), ln_g[i, 1], ln_b[i, 1], mod(i + 1, 1), mod(i + 1, 0),
                            None, lc=lc, alpha=alpha, out_h=BF16, packed_y=True)
    return s[None]
```

```python
import functools
import math

import jax
import jax.numpy as jnp
from jax import lax
from jax.experimental import pallas as pl
from jax.experimental.pallas import tpu as pltpu

HEAD_DIM = 128
GRID_W = 64
ROPE_THETA = 10000.0
POOL_WINDOWS = (2, 4, 8, 16)
POOL_HALO = 8
N_EXPERT_GROUPS = 4
EXPERTS_PER_GROUP = 4
N_EXPERTS = N_EXPERT_GROUPS * EXPERTS_PER_GROUP
N_MIXERS = 3
GQA_GROUP = 4
LN_EPS = 1e-6
RMS_EPS = 1e-6
LANES = 128
SUBLANES = 8
VMEM_LIMIT_BYTES = 56 * 1024 * 1024

F32 = jnp.float32
BF16 = jnp.bfloat16


def _params(*sem):
    return pltpu.CompilerParams(dimension_semantics=sem, vmem_limit_bytes=VMEM_LIMIT_BYTES)


def _tile(n, target, mult=SUBLANES):
    best = None
    for t in range(mult, min(n, target) + 1, mult):
        if n % t == 0:
            best = t
    assert best is not None, (n, target, mult)
    return best


def _sigmoid(v):
    return 1.0 / (1.0 + jnp.exp(-v))


def _pack_pairs(v):
    half = v.shape[1] // 2
    lo = lax.bitcast_convert_type(v[:, :half].astype(BF16).astype(F32), jnp.uint32)
    hi = lax.bitcast_convert_type(v[:, half:].astype(BF16).astype(F32), jnp.uint32)
    return hi | (lo >> 16)


def _unpack_pairs(u):
    lo = lax.bitcast_convert_type(u << 16, F32)
    hi = lax.bitcast_convert_type(u & jnp.uint32(0xFFFF0000), F32)
    return jnp.concatenate([lo, hi], axis=1)


def _ada_kernel(c_ref, w_ref, b_ref, o_ref, acc_ref, *, tk, tn):
    k = pl.program_id(2)

    @pl.when(k == 0)
    def _():
        acc_ref[...] = jnp.zeros_like(acc_ref)

    cv = c_ref[...]
    sv = cv * _sigmoid(cv)
    s0, s1 = sv[0], sv[1]
    for j in range(tn // LANES):
        cols = slice(j * LANES, (j + 1) * LANES)
        wj = w_ref[0, :, cols]
        acc_ref[0, :, cols] += (wj * s0).reshape(tk // SUBLANES, SUBLANES, LANES).sum(axis=0)
        acc_ref[1, :, cols] += (wj * s1).reshape(tk // SUBLANES, SUBLANES, LANES).sum(axis=0)

    @pl.when(k == pl.num_programs(2) - 1)
    def _():
        o_ref[0] = acc_ref[...].sum(axis=1) + b_ref[0]


def _ada_all(cvec, ada_w, ada_b):
    n_layers, d, n_out = ada_w.shape
    tk = _tile(d, 1024)
    tn = _tile(n_out, 2048, LANES)
    c_rep = jnp.broadcast_to(cvec[:, :, None], (2, d, LANES))
    return pl.pallas_call(
        functools.partial(_ada_kernel, tk=tk, tn=tn),
        grid=(n_layers, n_out // tn, d // tk),
        in_specs=[
            pl.BlockSpec((2, tk, LANES), lambda l, j, k: (0, k, 0)),
            pl.BlockSpec((1, tk, tn), lambda l, j, k: (l, k, j)),
            pl.BlockSpec((1, 1, tn), lambda l, j, k: (l, 0, j)),
        ],
        out_specs=pl.BlockSpec((1, 2, tn), lambda l, j, k: (l, 0, j)),
        out_shape=jax.ShapeDtypeStruct((n_layers, 2, n_out), F32),
        scratch_shapes=[pltpu.VMEM((2, SUBLANES, tn), F32)],
        compiler_params=_params("parallel", "parallel", "arbitrary"),
        name="ada_mod",
    )(c_rep, ada_w, ada_b.reshape(n_layers, 1, n_out))


def _route_rows(lg):
    ng, ne = N_EXPERT_GROUPS, EXPERTS_PER_GROUP
    g = [lg[r:r + 1, :] for r in range(ng)]
    gmax = functools.reduce(jnp.maximum, g)
    gidx = jnp.full(gmax.shape, ng - 1, jnp.int32)
    for r in range(ng - 2, -1, -1):
        gidx = jnp.where(g[r] == gmax, r, gidx)
    p_g = 1.0 / functools.reduce(jnp.add, [jnp.exp(gr - gmax) for gr in g])
    sel = []
    for j in range(ne):
        v = lg[ng + (ng - 1) * ne + j:ng + (ng - 1) * ne + j + 1, :]
        for r in range(ng - 2, -1, -1):
            v = jnp.where(gidx == r, lg[ng + r * ne + j:ng + r * ne + j + 1, :], v)
        sel.append(v)
    v1 = functools.reduce(jnp.maximum, sel)
    i1 = jnp.full(v1.shape, ne - 1, jnp.int32)
    for j in range(ne - 2, -1, -1):
        i1 = jnp.where(sel[j] == v1, j, i1)
    rest = [jnp.where(i1 == j, -jnp.inf, sel[j]) for j in range(ne)]
    v2 = functools.reduce(jnp.maximum, rest)
    i2 = jnp.full(v2.shape, ne - 1, jnp.int32)
    for j in range(ne - 2, -1, -1):
        i2 = jnp.where(rest[j] == v2, j, i2)
    t = jnp.exp(v2 - v1)
    w1 = p_g / (1.0 + t)
    w2 = p_g * t / (1.0 + t)
    ids = jnp.concatenate([gidx * ne + i1, gidx * ne + i2], axis=0)
    wts = jnp.concatenate([w1, w2], axis=0)
    return ids, wts


def _ln_mod_kernel(*refs, n_y, do_ln, out_h, do_route, alpha, packed_y, packed_h):
    refs = list(refs)
    x_ref = refs.pop(0)
    y_refs = [refs.pop(0) for _ in range(n_y)]
    if do_ln:
        gate_ref, lng_ref, lnb_ref = refs.pop(0), refs.pop(0), refs.pop(0)
    if out_h is not None:
        sc_ref, sh_ref = refs.pop(0), refs.pop(0)
    if do_route:
        whi_ref, wlo_ref, rb_ref = refs.pop(0), refs.pop(0), refs.pop(0)
    xo_ref = refs.pop(0) if do_ln else None
    h_ref = refs.pop(0) if out_h is not None else None
    if do_route:
        ids_ref, wts_ref = refs.pop(0), refs.pop(0)
    assert not refs

    x = x_ref[...]
    if do_ln:
        y = functools.reduce(jnp.add, [_unpack_pairs(r[...]) if packed_y else r[...] for r in y_refs])
        z = alpha * x + gate_ref[0] * y
        mu = jnp.mean(z, axis=-1, keepdims=True)
        zc = z - mu
        var = jnp.mean(zc * zc, axis=-1, keepdims=True)
        x = zc * lax.rsqrt(var + LN_EPS) * lng_ref[...] + lnb_ref[...]
        xo_ref[...] = x
    if out_h is not None:
        h = x * (1.0 + sc_ref[0]) + sh_ref[0]
        h_ref[...] = _pack_pairs(h) if packed_h else h.astype(out_h)
    if do_route:
        h_hi = h.astype(BF16)
        h_lo = (h - h_hi.astype(F32)).astype(BF16)
        w_hi = whi_ref[...]
        lg = (jnp.dot(h_hi, w_hi, preferred_element_type=F32)
              + jnp.dot(h_lo, w_hi, preferred_element_type=F32)
              + jnp.dot(h_hi, wlo_ref[...], preferred_element_type=F32)) + rb_ref[...]
        ids, wts = _route_rows(lg.T)
        pad = SUBLANES - ids.shape[0]
        ids_ref[...] = jnp.concatenate([ids, jnp.zeros((pad, ids.shape[1]), jnp.int32)], axis=0)
        wts_ref[...] = jnp.concatenate([wts, jnp.zeros((pad, wts.shape[1]), F32)], axis=0)


def _ln_mod(x, ys, gate, ln_g, ln_b, sc, sh, route_w, *, lc, alpha, out_h, row_offset=0,
            packed_y=False, packed_h=False):
    t_all, d = x.shape
    do_ln = len(ys) > 0
    do_route = route_w is not None
    tm = _tile(math.gcd(lc, t_all - lc), 256)
    assert row_offset % tm == 0
    off = row_offset // tm
    t_out = t_all - row_offset
    nct = lc // tm

    row = lambda i: (i + off, 0)
    stream = lambda i: (jnp.where(i + off < nct, 0, 1), 0, 0)
    vec = pl.BlockSpec((1, d), lambda i: (0, 0))
    svec = pl.BlockSpec((1, 1, d), stream)

    args, in_specs = [x], [pl.BlockSpec((tm, d), row)]
    n_y = 0
    for y in ys:
        for grp in range(y.shape[0] // t_all):
            n_y += 1
            args.append(y)
            in_specs.append(pl.BlockSpec((tm, y.shape[1]), lambda i, grp=grp: (i + off + grp * (t_all // tm), 0)))
    if do_ln:
        args += [gate, ln_g.reshape(1, d), ln_b.reshape(1, d)]
        in_specs += [svec, vec, vec]
    if out_h is not None:
        args += [sc, sh]
        in_specs += [svec, svec]
    if do_route:
        args += list(route_w)
        in_specs += [pl.BlockSpec((d, LANES), lambda i: (0, 0)),
                     pl.BlockSpec((d, LANES), lambda i: (0, 0)),
                     pl.BlockSpec((1, LANES), lambda i: (0, 0))]
    out_shape, out_specs = [], []
    if do_ln:
        out_shape.append(jax.ShapeDtypeStruct((t_out, d), F32))
        out_specs.append(pl.BlockSpec((tm, d), lambda i: (i, 0)))
    if out_h is not None:
        hd, hdt = (d // 2, jnp.uint32) if packed_h else (d, out_h)
        out_shape.append(jax.ShapeDtypeStruct((t_out, hd), hdt))
        out_specs.append(pl.BlockSpec((tm, hd), lambda i: (i, 0)))
    if do_route:
        out_shape += [jax.ShapeDtypeStruct((SUBLANES, t_out), jnp.int32),
                      jax.ShapeDtypeStruct((SUBLANES, t_out), F32)]
        out_specs += [pl.BlockSpec((SUBLANES, tm), lambda i: (0, i))] * 2
    return pl.pallas_call(
        functools.partial(_ln_mod_kernel, n_y=n_y, do_ln=do_ln, out_h=out_h,
                          do_route=do_route, alpha=alpha, packed_y=packed_y, packed_h=packed_h),
        grid=(t_out // tm,),
        in_specs=in_specs,
        out_specs=out_specs,
        out_shape=out_shape,
        compiler_params=_params("parallel"),
        name="ln_mod",
    )(*args)


def _proj_kernel(*refs, norm, rope, scale, tn):
    refs = list(refs)
    a_ref, w_ref = refs.pop(0), refs.pop(0)
    g_ref = refs.pop(0) if norm else None
    if rope:
        cos_ref, sin_ref = refs.pop(0), refs.pop(0)
    o_ref = refs.pop(0)
    acc = jnp.dot(a_ref[...], w_ref[...], preferred_element_type=F32)
    if not (norm or rope):
        if scale != 1.0:
            acc = acc * scale
        o_ref[...] = acc.astype(o_ref.dtype)
        return
    if rope:
        cos, sin = cos_ref[...], sin_ref[...]
        lane = lax.broadcasted_iota(jnp.int32, cos.shape, 1)
        first_half = (lane % (HEAD_DIM // 2)) < (HEAD_DIM // 4)
    for hh in range(tn // HEAD_DIM):
        cols = slice(hh * HEAD_DIM, (hh + 1) * HEAD_DIM)
        t = acc[:, cols]
        if norm:
            t = t * lax.rsqrt(jnp.mean(t * t, axis=-1, keepdims=True) + RMS_EPS) * g_ref[...]
        if rope:
            up = pltpu.roll(t, HEAD_DIM - HEAD_DIM // 4, 1)
            dn = pltpu.roll(t, HEAD_DIM // 4, 1)
            t = t * cos + jnp.where(first_half, up, dn) * sin
        if scale != 1.0:
            t = t * scale
        o_ref[:, cols] = t.astype(o_ref.dtype)


def _proj(a, w, layer, *, out_dtype, norm_g=None, rope=None, scale=1.0, tm_target=768, tn_target=1024):
    t_all, kdim = a.shape
    n = w.shape[2]
    tm = _tile(t_all, tm_target)
    tn = _tile(n, tn_target, LANES)
    args = [a, w]
    in_specs = [pl.BlockSpec((tm, kdim), lambda i, j: (i, 0)),
                pl.BlockSpec((None, kdim, tn), lambda i, j: (layer, 0, j))]
    if norm_g is not None:
        args.append(norm_g.reshape(1, HEAD_DIM))
        in_specs.append(pl.BlockSpec((1, HEAD_DIM), lambda i, j: (0, 0)))
    if rope is not None:
        args += list(rope)
        in_specs += [pl.BlockSpec((tm, HEAD_DIM), lambda i, j: (i, 0))] * 2
    return pl.pallas_call(
        functools.partial(_proj_kernel, norm=norm_g is not None, rope=rope is not None,
                          scale=scale, tn=tn),
        grid=(t_all // tm, n // tn),
        in_specs=in_specs,
        out_specs=pl.BlockSpec((tm, tn), lambda i, j: (i, j)),
        out_shape=jax.ShapeDtypeStruct((t_all, n), out_dtype),
        compiler_params=_params("parallel", "parallel"),
        name="proj",
    )(*args)


def _proj_rope_kernel(*refs, norm, scale, tn):
    refs = list(refs)
    a_ref, w_ref = refs.pop(0), refs.pop(0)
    g_ref = refs.pop(0) if norm else None
    cos_ref, sin_ref, o_ref, acc0, acc1 = refs
    s = pl.program_id(0)

    @pl.when(s == 0)
    def _():
        acc1[...] = jnp.zeros_like(acc1)

    def epilogue(acc_ref):
        cos, sin = cos_ref[...], sin_ref[...]
        lane = lax.broadcasted_iota(jnp.int32, cos.shape, 1)
        first_half = (lane % (HEAD_DIM // 2)) < (HEAD_DIM // 4)
        for hh in range(tn // HEAD_DIM):
            cols = slice(hh * HEAD_DIM, (hh + 1) * HEAD_DIM)
            t = acc_ref[:, cols]
            if norm:
                t = t * lax.rsqrt(jnp.mean(t * t, axis=-1, keepdims=True) + RMS_EPS) * g_ref[...]
            up = pltpu.roll(t, HEAD_DIM - HEAD_DIM // 4, 1)
            dn = pltpu.roll(t, HEAD_DIM // 4, 1)
            t = t * cos + jnp.where(first_half, up, dn) * sin
            if scale != 1.0:
                t = t * scale
            o_ref[:, cols] = t.astype(o_ref.dtype)

    def step(dst, src):
        dst[...] = jnp.dot(a_ref[...], w_ref[...], preferred_element_type=F32)
        epilogue(src)

    pl.when(s % 2 == 0)(lambda: step(acc0, acc1))
    pl.when(s % 2 == 1)(lambda: step(acc1, acc0))


def _proj_rope(a, w, layer, rope, *, norm_g=None, scale=1.0, tm_target=768, tn_target=1024):
    t_all, kdim = a.shape
    n = w.shape[2]
    tm = _tile(t_all, tm_target)
    tn = _tile(n, tn_target, LANES)
    n_j = n // tn
    n_tiles = (t_all // tm) * n_j
    mm = lambda s: jnp.minimum(s, n_tiles - 1)
    ep = lambda s: jnp.maximum(s - 1, 0)
    args = [a, w]
    in_specs = [pl.BlockSpec((tm, kdim), lambda s: (mm(s) // n_j, 0)),
                pl.BlockSpec((None, kdim, tn), lambda s: (layer, 0, mm(s) % n_j))]
    if norm_g is not None:
        args.append(norm_g.reshape(1, HEAD_DIM))
        in_specs.append(pl.BlockSpec((1, HEAD_DIM), lambda s: (0, 0)))
    args += list(rope)
    in_specs += [pl.BlockSpec((tm, HEAD_DIM), lambda s: (ep(s) // n_j, 0))] * 2
    return pl.pallas_call(
        functools.partial(_proj_rope_kernel, norm=norm_g is not None, scale=scale, tn=tn),
        grid=(n_tiles + 1,),
        in_specs=in_specs,
        out_specs=pl.BlockSpec((tm, tn), lambda s: (ep(s) // n_j, ep(s) % n_j)),
        out_shape=jax.ShapeDtypeStruct((t_all, n), BF16),
        scratch_shapes=[pltpu.VMEM((tm, tn), F32)] * 2,
        compiler_params=_params("arbitrary"),
        name="proj_rope",
    )(*args)


def _rope_tables(n, lc):
    n_freq = HEAD_DIM // 4
    pos = jnp.arange(n, dtype=jnp.int32)
    row = (pos // GRID_W).astype(F32)
    col = (pos % GRID_W).astype(F32)
    inv_freq = ROPE_THETA ** (-jnp.arange(n_freq, dtype=F32) / n_freq)
    ar, ac = row[:, None] * inv_freq, col[:, None] * inv_freq
    cos = jnp.concatenate([jnp.cos(ar), jnp.cos(ar), jnp.cos(ac), jnp.cos(ac)], axis=1)
    sin = jnp.concatenate([-jnp.sin(ar), jnp.sin(ar), -jnp.sin(ac), jnp.sin(ac)], axis=1)
    cos = jnp.concatenate([jnp.ones((lc, HEAD_DIM), F32), cos], axis=0)
    sin = jnp.concatenate([jnp.zeros((lc, HEAD_DIM), F32), sin], axis=0)
    return cos, sin


def _sweep(qs, kcols, k_ref, v_ref, latent_rows, finish, *, lc, tk, n_lat, dv, sum_on_mxu):
    nq = len(qs)
    n = n_lat // tk

    def e0(size):
        return (lax.broadcasted_iota(jnp.int32, (size, LANES), 1) == 0).astype(v_ref.dtype)

    ones_c, ones_t = (e0(lc), e0(tk)) if sum_on_mxu else (None, None)

    def scores(lo, size):
        return [lax.dot_general(q, k_ref[pl.ds(lo, size), kc], (((1,), (1,)), ((), ())),
                                preferred_element_type=F32) for q, kc in zip(qs, kcols)]

    def pv(p, lo, size, ones):
        vc = v_ref[pl.ds(lo, size), :]
        if sum_on_mxu:
            vc = jnp.concatenate([vc, ones], axis=1)
        return jnp.dot(p, vc, preferred_element_type=F32)

    def soft(s, m, l):
        m_new = jnp.maximum(m, jnp.max(s, axis=-1, keepdims=True))
        a = jnp.exp2(m - m_new)
        p = jnp.exp2(s - m_new)
        if not sum_on_mxu:
            l = a * l + jnp.sum(p, axis=-1, keepdims=True)
        return m_new, a, p.astype(v_ref.dtype), l

    def result(acc, l):
        return acc[:, :dv] / acc[:, dv:dv + 1] if sum_on_mxu else acc / l

    def context():
        out = []
        for s in scores(0, lc):
            m = jnp.max(s, axis=-1, keepdims=True)
            p = jnp.exp2(s - m)
            l = None if sum_on_mxu else jnp.sum(p, axis=-1, keepdims=True)
            out.append((m, l, pv(p.astype(v_ref.dtype), 0, lc, ones_c)))
        return out

    @pl.when(jnp.logical_not(latent_rows))
    def _():
        finish([result(acc, l) for _, l, acc in context()])

    @pl.when(latent_rows)
    def _():
        ctx = context()

        def start(c, s):
            m0, l0, acc0 = ctx[c]
            m, a, p, l = soft(s, m0, l0)
            return (m, p, a * acc0) if sum_on_mxu else (m, l, p, a * acc0)

        state = tuple(start(c, s) for c, s in enumerate(scores(lc, tk)))

        def body(j, state):
            lo = pl.multiple_of(lc + j * tk, math.gcd(lc, tk))
            ss = scores(lo, tk)
            new = []
            for c in range(nq):
                m0, l0 = state[c][0], None if sum_on_mxu else state[c][1]
                acc = state[c][-1] + pv(state[c][-2], lo - tk, tk, ones_t)
                m, a, p, l = soft(ss[c], m0, l0)
                new.append((m, p, a * acc) if sum_on_mxu else (m, l, p, a * acc))
            return tuple(new)

        state = lax.fori_loop(1, n, body, state, unroll=True)
        outs = []
        for c in range(nq):
            l = None if sum_on_mxu else state[c][1]
            p, r = state[c][-2], state[c][-1]
            outs.append(result(r + pv(p, lc + (n - 1) * tk, tk, ones_t), l))
        finish(outs)


def _gqa_kernel(q_ref, k_ref, v_ref, o_ref, *, lc, tq, tk, n_lat):
    i = pl.program_id(1)
    heads = [slice(h * HEAD_DIM, (h + 1) * HEAD_DIM) for h in range(GQA_GROUP)]

    def finish(outs):
        for cols, o in zip(heads, outs):
            o_ref[:, cols] = o.astype(o_ref.dtype)

    _sweep([q_ref[:, cols] for cols in heads], [heads[0]] * GQA_GROUP, k_ref, v_ref, i * tq >= lc, finish,
           lc=lc, tk=tk, n_lat=n_lat, dv=HEAD_DIM, sum_on_mxu=True)


def _gqa_attention(q, k, v, *, lc, tq_target=256, tk_target=2048):
    t_all, dq = q.shape
    n_kv = k.shape[1] // HEAD_DIM
    n_lat = t_all - lc
    tq = _tile(math.gcd(lc, n_lat), tq_target)
    tk = _tile(n_lat, tk_target)
    gw = GQA_GROUP * HEAD_DIM
    return pl.pallas_call(
        functools.partial(_gqa_kernel, lc=lc, tq=tq, tk=tk, n_lat=n_lat),
        grid=(n_kv, t_all // tq),
        in_specs=[pl.BlockSpec((tq, gw), lambda g, i: (i, g)),
                  pl.BlockSpec((t_all, HEAD_DIM), lambda g, i: (0, g)),
                  pl.BlockSpec((t_all, HEAD_DIM), lambda g, i: (0, g))],
        out_specs=pl.BlockSpec((tq, gw), lambda g, i: (i, g)),
        out_shape=jax.ShapeDtypeStruct((t_all, dq), BF16),
        compiler_params=_params("parallel", "parallel"),
        name="gqa_attention",
    )(q, k, v)


def _diff_kernel(q_ref, k_ref, v_ref, lq1_ref, lk1_ref, lq2_ref, lk2_ref, g_ref, o_ref,
                 *, lc, tq, tk, n_lat, lam_init):
    i = pl.program_id(1)
    dv = 2 * HEAD_DIM

    def finish(outs):
        lam = (jnp.exp(jnp.sum(lq1_ref[...] * lk1_ref[...], axis=-1, keepdims=True))
               - jnp.exp(jnp.sum(lq2_ref[...] * lk2_ref[...], axis=-1, keepdims=True)) + lam_init)
        o = outs[0] - lam * outs[1]
        o = o * lax.rsqrt(jnp.mean(o * o, axis=-1, keepdims=True) + RMS_EPS) * g_ref[...]
        o_ref[...] = (o * (1.0 - lam_init)).astype(o_ref.dtype)

    halves = [slice(0, HEAD_DIM), slice(HEAD_DIM, dv)]
    _sweep([q_ref[:, cols] for cols in halves], halves, k_ref, v_ref, i * tq >= lc, finish,
           lc=lc, tk=tk, n_lat=n_lat, dv=dv, sum_on_mxu=False)


def _diff_attention(q, k, v, lq1, lk1, lq2, lk2, subln_g, *, lc, lam_init, tq_target=256, tk_target=1024):
    t_all, dq = q.shape
    dv = 2 * HEAD_DIM
    n_heads = dq // dv
    n_lat = t_all - lc
    tq = _tile(math.gcd(lc, n_lat), tq_target)
    tk = _tile(n_lat, tk_target)
    vec = lambda n: pl.BlockSpec((1, n), lambda h, i: (0, 0))
    return pl.pallas_call(
        functools.partial(_diff_kernel, lc=lc, tq=tq, tk=tk, n_lat=n_lat, lam_init=lam_init),
        grid=(n_heads, t_all // tq),
        in_specs=[pl.BlockSpec((tq, dv), lambda h, i: (i, h)),
                  pl.BlockSpec((t_all, dv), lambda h, i: (0, h)),
                  pl.BlockSpec((t_all, dv), lambda h, i: (0, h)),
                  vec(HEAD_DIM), vec(HEAD_DIM), vec(HEAD_DIM), vec(HEAD_DIM), vec(dv)],
        out_specs=pl.BlockSpec((tq, dv), lambda h, i: (i, h)),
        out_shape=jax.ShapeDtypeStruct((t_all, v.shape[1]), BF16),
        compiler_params=_params("parallel", "parallel"),
        name="diff_attention",
    )(q, k, v, lq1.reshape(1, -1), lk1.reshape(1, -1), lq2.reshape(1, -1), lk2.reshape(1, -1),
      subln_g.reshape(1, -1))


def _pool_kernel(x_ref, xp_ref, xn_ref, sc_ref, sh_ref, w_ref, b_ref, ps_ref, o_ref, buf_ref,
                 *, lc, n_lat, tm):
    i = pl.program_id(0)
    nct = lc // tm
    nt = (lc + n_lat) // tm
    sc, sh = sc_ref[0], sh_ref[0]
    first = jnp.logical_or(i == 0, i == nct)
    last = jnp.logical_or(i == nct - 1, i == nt - 1)
    cur = x_ref[...] * (1.0 + sc) + sh
    buf_ref[0:POOL_HALO, :] = jnp.where(first, 0.0, xp_ref[...] * (1.0 + sc) + sh)
    buf_ref[POOL_HALO:POOL_HALO + tm, :] = cur
    buf_ref[POOL_HALO + tm:2 * POOL_HALO + tm, :] = jnp.where(last, 0.0, xn_ref[...] * (1.0 + sc) + sh)

    in_ctx = i < nct
    t_loc = (lax.broadcasted_iota(jnp.int32, (tm, 1), 0)
             + (i - jnp.where(in_ctx, 0, nct)) * tm)
    n_s = jnp.where(in_ctx, lc, n_lat)
    cdim = w_ref.shape[1]
    for g, win in enumerate(POOL_WINDOWS):
        cols = slice(g * cdim, (g + 1) * cdim)
        half = win // 2
        acc = buf_ref[POOL_HALO - half:POOL_HALO - half + tm, cols]
        for k in range(-half + 1, half):
            acc = acc + buf_ref[POOL_HALO + k:POOL_HALO + k + tm, cols]
        cnt = jnp.minimum(t_loc + half - 1, n_s - 1) - jnp.maximum(t_loc - half, 0) + 1
        y = acc / cnt.astype(F32) - cur[:, cols]
        out = jnp.dot(y.astype(BF16), w_ref[g], preferred_element_type=F32) + b_ref[g]
        o_ref[:, cols] = out * ps_ref[:, cols]


def _pool_mixer(x, sc, sh, pool_w, pool_b, pool_scale, *, lc):
    t_all, d = x.shape
    n_lat = t_all - lc
    ng, cdim, _ = pool_w.shape
    tm = _tile(math.gcd(lc, n_lat), 128)
    hb = tm // POOL_HALO
    n_hb = t_all // POOL_HALO
    stream = lambda i: (jnp.where(i < lc // tm, 0, 1), 0, 0)
    return pl.pallas_call(
        functools.partial(_pool_kernel, lc=lc, n_lat=n_lat, tm=tm),
        grid=(t_all // tm,),
        in_specs=[pl.BlockSpec((tm, d), lambda i: (i, 0)),
                  pl.BlockSpec((POOL_HALO, d), lambda i: (jnp.maximum(i * hb - 1, 0), 0)),
                  pl.BlockSpec((POOL_HALO, d), lambda i: (jnp.minimum((i + 1) * hb, n_hb - 1), 0)),
                  pl.BlockSpec((1, 1, d), stream),
                  pl.BlockSpec((1, 1, d), stream),
                  pl.BlockSpec((ng, cdim, cdim), lambda i: (0, 0, 0)),
                  pl.BlockSpec((ng, 1, cdim), lambda i: (0, 0, 0)),
                  pl.BlockSpec((1, d), lambda i: (0, 0))],
        out_specs=pl.BlockSpec((tm, d), lambda i: (i, 0)),
        out_shape=jax.ShapeDtypeStruct((t_all, d), F32),
        scratch_shapes=[pltpu.VMEM((tm + 2 * POOL_HALO, d), F32)],
        compiler_params=_params("parallel"),
        name="pool_mixer",
    )(x, x, x, sc, sh, pool_w.astype(BF16), pool_b.reshape(ng, 1, cdim), pool_scale.reshape(1, d))


def _moe_kernel(texp_ref, rtok_ref, rdst_ref, nact_ref,
                h_hbm, gate_ref, w1_ref, w3_ref, w2_ref, y_hbm,
                xbuf, ybuf, sem_in, sem_out, *, tm, n_tiles):
    i = pl.program_id(0)
    n = nact_ref[0]
    slot = lax.rem(i, 2)
    spare = y_hbm.shape[0] - 2 * tm

    def gather(tile, s, k):
        tok = rtok_ref[tile * tm + k]
        return pltpu.make_async_copy(h_hbm.at[pl.ds(tok, 1)], xbuf.at[s, pl.ds(k, 1)], sem_in.at[s])

    def scatter(tile, s, k):
        dst = rdst_ref[(tile + 1) * tm + k]
        return pltpu.make_async_copy(ybuf.at[s, pl.ds(k, 1)], y_hbm.at[pl.ds(dst, 1)], sem_out.at[s])

    def fill(k):
        return pltpu.make_async_copy(ybuf.at[0, pl.ds(k, 1)], y_hbm.at[pl.ds(spare + k, 1)], sem_out.at[0])

    @pl.when(i == 0)
    def _():
        ybuf[...] = jnp.zeros_like(ybuf)
        for k in range(tm):
            fill(k).start()
        for k in range(tm):
            gather(0, 0, k).start()
        for k in range(tm):
            fill(k).wait()

    @pl.when(jnp.logical_and(i >= 1, i < n))
    def _():
        for k in range(tm):
            scatter(i - 2, slot, k).wait()

    @pl.when(i < n)
    def _():
        for k in range(tm):
            gather(i, slot, k).wait()
        nxt = jnp.minimum(i + 1, n_tiles - 1)
        x = _unpack_pairs(xbuf[slot]).astype(BF16)
        for k in range(tm):
            gather(nxt, 1 - slot, k).start(priority=k % 2)
        a = jnp.dot(x, w1_ref[0], preferred_element_type=F32)
        u = jnp.dot(x, w3_ref[0], preferred_element_type=F32)
        for k in range(tm):
            scatter(i - 1, 1 - slot, k).start(priority=k % 2)
        hid = a * _sigmoid(a) * u * gate_ref[...]
        ybuf[slot] = _pack_pairs(jnp.dot(hid.astype(BF16), w2_ref[0], preferred_element_type=F32))

        @pl.when(i == n - 1)
        def _():
            for k in range(tm):
                scatter(i, slot, k).start()
            for k in range(tm):
                gather(nxt, 1 - slot, k).wait()
            for k in range(tm):
                scatter(i - 1, 1 - slot, k).wait()
            for k in range(tm):
                scatter(i, slot, k).wait()


def _moe(h, ids, wts, w1, w3, w2, layer, *, tm_target=256):
    _, n_exp, d, f = w1.shape
    t_all = h.shape[0]
    tm = _tile(t_all, tm_target)
    n_pairs = 2 * t_all
    n_tiles = n_pairs // tm + n_exp
    n_rows = n_tiles * tm

    e_flat = ids.reshape(n_pairs)
    w_flat = wts.reshape(n_pairs)
    onehot = (e_flat[:, None] == jnp.arange(n_exp, dtype=jnp.int32)[None, :]).astype(jnp.int32)
    rank = jnp.sum((jnp.cumsum(onehot, axis=0) - onehot) * onehot, axis=1)
    counts = jnp.sum(onehot, axis=0)
    padded = (counts + tm - 1) // tm * tm
    p_end = jnp.cumsum(padded)
    pos = jnp.sum(onehot * (p_end - padded)[None, :], axis=1) + rank
    row_pair = jnp.full((n_rows,), -1, jnp.int32).at[pos].set(jnp.arange(n_pairs, dtype=jnp.int32))
    valid = row_pair >= 0
    safe = jnp.maximum(row_pair, 0)
    row_tok = jnp.where(valid, safe % t_all, 0)
    row_gate = jnp.where(valid, w_flat[safe], 0.0).reshape(n_rows, 1)
    rows = jnp.arange(-tm, n_rows, dtype=jnp.int32)
    spare_row = n_pairs + (rows // tm) % 2 * tm + rows % tm
    row_dst = jnp.where(rows >= 0, jnp.where(jnp.concatenate([jnp.zeros((tm,), bool), valid]),
                                             jnp.concatenate([jnp.zeros((tm,), jnp.int32), row_pair]),
                                             spare_row), spare_row)
    n_act = (p_end[-1] // tm).astype(jnp.int32)
    tile_start = jnp.arange(n_tiles, dtype=jnp.int32) * tm
    tile_exp = jnp.sum(p_end[None, :] <= tile_start[:, None], axis=1, dtype=jnp.int32)
    last_exp = jnp.sum(p_end <= (n_act - 1) * tm, dtype=jnp.int32)
    tile_exp = jnp.minimum(jnp.where(tile_start < n_act * tm, tile_exp, last_exp), n_exp - 1)

    wspec = lambda shape: pl.BlockSpec((None,) + shape, lambda i, te, rt, rd, na: (layer, te[i], 0, 0))
    return pl.pallas_call(
        functools.partial(_moe_kernel, tm=tm, n_tiles=n_tiles),
        grid_spec=pltpu.PrefetchScalarGridSpec(
            num_scalar_prefetch=4,
            grid=(n_tiles,),
            in_specs=[pl.BlockSpec(memory_space=pl.ANY),
                      pl.BlockSpec((tm, 1), lambda i, te, rt, rd, na: (i, 0)),
                      wspec((1, d, f)), wspec((1, d, f)), wspec((1, f, d))],
            out_specs=pl.BlockSpec(memory_space=pl.ANY),
            scratch_shapes=[pltpu.VMEM((2, tm, d // 2), jnp.uint32)] * 2 + [
                            pltpu.SemaphoreType.DMA((2,)), pltpu.SemaphoreType.DMA((2,))],
        ),
        out_shape=jax.ShapeDtypeStruct((n_pairs + 2 * tm, d // 2), jnp.uint32),
        compiler_params=_params("arbitrary"),
        name="moe_experts",
    )(tile_exp, row_tok, row_dst, n_act.reshape(1), h, row_gate,
      w1, w3, w2)


def _router_weights(rg_w, rg_b, re_w, re_b):
    d = rg_w.shape[0]
    n = rg_w.shape[1] + re_w.shape[1]
    w = jnp.concatenate([rg_w, re_w, jnp.zeros((d, LANES - n), F32)], axis=1)
    b = jnp.concatenate([rg_b, re_b, jnp.zeros((LANES - n,), F32)]).reshape(1, LANES)
    w_hi = w.astype(BF16)
    w_lo = (w - w_hi.astype(F32)).astype(BF16)
    return w_hi, w_lo, b


def kernel(x, c, ctx, c_ctx, ada_w, ada_b, ln_g, ln_b, attn_wq, attn_wk, attn_wv, attn_wo, attn_qn_g, attn_kn_g, pool_w, pool_b, pool_scale, diff_wq, diff_wk, diff_wv, diff_wo, diff_lq1, diff_lk1, diff_lq2, diff_lk2, diff_subln_g, moe_rg_w, moe_rg_b, moe_re_w, moe_re_b, moe_w1, moe_w3, moe_w2):
    b, n, d = x.shape
    assert b == 1 and c.shape[0] == 1 and ctx.shape[0] == 1
    lc = ctx.shape[1]
    depth = ada_w.shape[0]
    alpha = (2 * depth) ** 0.25
    qk_scale = math.log2(math.e) / math.sqrt(HEAD_DIM)

    s = jnp.concatenate([ctx[0], x[0]], axis=0)
    mods = _ada_all(jnp.concatenate([c_ctx[None, :], c], axis=0), ada_w, ada_b)
    mod = lambda i, k: mods[i, :, k * d:(k + 1) * d].reshape(2, 1, d)
    rope = _rope_tables(n, lc)
    bf = lambda w: w.astype(BF16)
    attn_w = [bf(w) for w in (attn_wq, attn_wk, attn_wv, attn_wo)]
    diff_w = [bf(w) for w in (diff_wq, diff_wk, diff_wv, diff_wo)]
    moe_w = [bf(w) for w in (moe_w1, moe_w3, moe_w2)]

    h1 = None
    for i in range(depth):
        last = i == depth - 1
        kind, j = i % N_MIXERS, i // N_MIXERS
        if kind != 1 and h1 is None:
            (h1,) = _ln_mod(s, (), None, None, None, mod(i, 1), mod(i, 0), None,
                            lc=lc, alpha=alpha, out_h=BF16)
        if kind == 0:
            q = _proj_rope(h1, attn_w[0], j, rope, norm_g=attn_qn_g[j], scale=qk_scale)
            k = _proj_rope(h1, attn_w[1], j, rope, norm_g=attn_kn_g[j])
            v = _proj(h1, attn_w[2], j, out_dtype=BF16)
            o = _gqa_attention(q, k, v, lc=lc)
            y = _proj(o, attn_w[3], j, out_dtype=F32)
        elif kind == 1:
            y = _pool_mixer(s, mod(i, 1), mod(i, 0), pool_w[j], pool_b[j], pool_scale[j], lc=lc)
        else:
            lam_init = 0.8 - 0.6 * math.exp(-0.3 * i)
            q = _proj_rope(h1, diff_w[0], j, rope, scale=qk_scale)
            k = _proj_rope(h1, diff_w[1], j, rope)
            v = _proj(h1, diff_w[2], j, out_dtype=BF16)
            o = _diff_attention(q, k, v, diff_lq1[j], diff_lk1[j], diff_lq2[j], diff_lk2[j],
                                diff_subln_g[j], lc=lc, lam_init=lam_init)
            y = _proj(o, diff_w[3], j, out_dtype=F32)
        h1 = None
        route_w = _router_weights(moe_rg_w[i], moe_rg_b[i], moe_re_w[i], moe_re_b[i])
        s, h2, ids, wts = _ln_mod(s, (y,), mod(i, 2), ln_g[i, 0], ln_b[i, 0], mod(i, 4), mod(i, 3),
                                  route_w, lc=lc, alpha=alpha, out_h=F32, packed_h=True)
        y2 = _moe(h2, ids[:2], wts[:2], *moe_w, i)
        if last:
            (s,) = _ln_mod(s, (y2,), mod(i, 5), ln_g[i, 1], ln_b[i, 1], None, None, None,
                           lc=lc, alpha=alpha, out_h=None, row_offset=lc, packed_y=True)
        elif (i + 1) % N_MIXERS == 1:
            (s,) = _ln_mod(s, (y2,), mod(i, 5), ln_g[i, 1], ln_b[i, 1], None, None, None,
                           lc=lc, alpha=alpha, out_h=None, packed_y=True)
        else:
            s, h1 = _ln_mod(s, (y2,), mod(i, 5), ln_g[i, 1], ln_b[i, 1], mod(i + 1, 1), mod(i + 1, 0),
                            None, lc=lc, alpha=alpha, out_h=BF16, packed_y=True)
    return s[None]
```
